```python
import math
import jax
import jax.numpy as jnp
from jax import lax
import numpy as np

D_MODEL = 1024
BATCH = 32
SEQ = 256
DEPTH = 4
DEC_BATCH = 2
DEC_SEQ = 4096
PAST_LEN = 512

GRID_W = 64
N_GROUPS = 4
GROUP_W = D_MODEL // N_GROUPS
HEAD_DIM = 64
F32 = jnp.float32
HY_W = GROUP_W
HY_COLS = 3 * HY_W
HY_BANDS = 16
HY_EMB = 2 * HY_BANDS + 1
HY_FFN = 64
HY_TARGET = 1e-2
HY_MAX_DECAY = math.log(HY_TARGET) / 0.3
HY_MIN_DECAY = math.log(HY_TARGET) / 1.5
ML_HEADS = GROUP_W // HEAD_DIM
ML_CHUNK = 128
ML_COLS = 4 * GROUP_W + 4 * ML_HEADS
RW_HEADS = GROUP_W // HEAD_DIM
RW_W_RANK = 64
RW_A_RANK = 64
RW_G_RANK = 128
RW_DECAY_SCALE = 0.606531
RW_GN_EPS = 64e-5
RW_COLS = 3 * GROUP_W + RW_W_RANK + RW_A_RANK + RW_G_RANK
AT_HEADS = GROUP_W // HEAD_DIM
AT_KV_HEADS = 2
AT_BLOCK = 128
AT_COLS = (AT_HEADS + 2 * AT_KV_HEADS) * HEAD_DIM
ROPE_THETA = 10000.0
ROPE_AXIS = HEAD_DIM // 2
ROPE_FREQS = ROPE_AXIS // 2
P_IN = HY_COLS + ML_COLS + RW_COLS + AT_COLS
IN_SPLITS = (HY_COLS, HY_COLS + ML_COLS, HY_COLS + ML_COLS + RW_COLS)
ML_SPLITS = (GROUP_W, 2 * GROUP_W, 3 * GROUP_W, 4 * GROUP_W)
RW_SPLITS = (GROUP_W, 2 * GROUP_W, 3 * GROUP_W, 3 * GROUP_W + RW_W_RANK, 3 * GROUP_W + RW_W_RANK + RW_A_RANK)
AT_SPLITS = (AT_HEADS * HEAD_DIM, (AT_HEADS + AT_KV_HEADS) * HEAD_DIM)
D_FF = 4 * D_MODEL
ALPHA = (2.0 * DEPTH) ** 0.25
BETA = (8.0 * DEPTH) ** -0.25

kernel_name = 'hybrid_flow_hymba_step'


def _prev(x):
    return jnp.pad(x, ((0, 0), (1, 0), (0, 0)))[:, :-1]


def _next(x):
    return jnp.pad(x, ((0, 0), (0, 1), (0, 0)))[:, 1:]


def _dwconv3(x, w):
    return _prev(x) * w[0] + x * w[1] + _next(x) * w[2]


def _layernorm(x, g, b, eps=1e-5):
    xf = x.astype(F32)
    mu = jnp.mean(xf, -1, keepdims=True)
    var = jnp.mean(jnp.square(xf - mu), -1, keepdims=True)
    return ((xf - mu) * lax.rsqrt(var + eps) * g.astype(F32) + b.astype(F32)).astype(x.dtype)


def _rmsnorm(x, g, eps=1e-6):
    xf = x.astype(F32)
    return (xf * lax.rsqrt(jnp.mean(jnp.square(xf), -1, keepdims=True) + eps) * g.astype(F32)).astype(x.dtype)


def _head_norm(y, eps):
    mu = jnp.mean(y, -1, keepdims=True)
    var = jnp.mean(jnp.square(y - mu), -1, keepdims=True)
    return (y - mu) * lax.rsqrt(var + eps)


def _hyena_filters(L, w1, b1, freq, w2, b2, w3):
    t = jnp.linspace(0.0, 1.0, L, dtype=F32)[:, None]
    ang = (2.0 * math.pi / L) * jnp.arange(L, dtype=F32)[:, None]
    bands = jnp.linspace(1e-4, HY_BANDS - 1, HY_BANDS, dtype=F32)[None, :]
    z = jnp.concatenate([t, jnp.cos(bands * ang), jnp.sin(bands * ang)], axis=-1)
    fr = freq.astype(F32)
    hid = jnp.sin(fr * (z @ w1.astype(F32) + b1.astype(F32)))
    hid = jnp.sin(fr * (hid @ w2.astype(F32) + b2.astype(F32)))
    h = (hid @ w3.astype(F32)).reshape(L, 2, 2, HY_W)
    deltas = jnp.abs(jnp.linspace(HY_MIN_DECAY, HY_MAX_DECAY, HY_W, dtype=F32))
    h = h * jnp.exp(-t * deltas)[:, None, None, :]
    causal = h[:, :, 0]
    anti = h[1:, :, 1][::-1]
    filt = jnp.concatenate([causal, jnp.zeros((1, 2, HY_W), F32), anti], axis=0)
    return filt / jnp.sum(jnp.abs(filt), axis=0, keepdims=True)


def _long_conv(u, filt, bias):
    L = u.shape[1]
    uf = u.astype(F32)
    spec = jnp.fft.rfft(uf, n=2 * L, axis=1) * jnp.fft.rfft(filt, axis=0)[None]
    y = jnp.fft.irfft(spec, n=2 * L, axis=1)[:, :L]
    return (y + uf * bias.astype(F32)).astype(u.dtype)


def _hyena(p, conv_w, filt, bias):
    p = _dwconv3(p, conv_w)
    v, x1, x2 = jnp.split(p, 3, axis=-1)
    z = x1 * _long_conv(v, filt[:, 0], bias[0])
    return x2 * _long_conv(z, filt[:, 1], bias[1])


def _mlstm_scan(q, k, v, log_i, log_f, C0, n0, m0):
    B, L, H, Dh = q.shape
    nc = L // ML_CHUNK

    def to_chunks(a):
        a = a.reshape((B, nc, ML_CHUNK) + a.shape[2:])
        return jnp.moveaxis(jnp.moveaxis(a, 1, 0), 3, 2)

    tril = jnp.tril(jnp.ones((ML_CHUNK, ML_CHUNK), bool))

    def step(carry, inp):
        C, n, m = carry
        qc, kc, vc, ic, fc = inp
        b = jnp.cumsum(fc, axis=-1)
        logd = jnp.where(tril, b[..., :, None] - b[..., None, :] + ic[..., None, :], -jnp.inf)
        inter = b + m[..., None]
        m_t = jnp.maximum(inter, jnp.max(logd, axis=-1))
        s = jnp.einsum('bhtd,bhsd->bhts', qc, kc) * jnp.exp(logd - m_t[..., None])
        w_in = jnp.exp(inter - m_t)
        num = jnp.einsum('bhts,bhse->bhte', s, vc) + w_in[..., None] * jnp.einsum('bhtd,bhde->bhte', qc, C)
        den = jnp.sum(s, -1) + w_in * jnp.einsum('bhtd,bhd->bht', qc, n)
        h = num / jnp.maximum(jnp.abs(den), jnp.exp(-m_t))[..., None]
        b_last = b[..., -1]
        logw = b_last[..., None] - b + ic
        m_new = jnp.maximum(b_last + m, jnp.max(logw, axis=-1))
        wk = jnp.exp(logw - m_new[..., None])
        decay = jnp.exp(b_last + m - m_new)
        C_new = decay[..., None, None] * C + jnp.einsum('bhs,bhsd,bhse->bhde', wk, kc, vc)
        n_new = decay[..., None] * n + jnp.einsum('bhs,bhsd->bhd', wk, kc)
        return (C_new, n_new, m_new), h

    xs = (to_chunks(q), to_chunks(k), to_chunks(v), to_chunks(log_i), to_chunks(log_f))
    (C, n, m), h = lax.scan(step, (C0, n0, m0), xs)
    h = jnp.swapaxes(jnp.moveaxis(h, 0, 1), 2, 3).reshape(B, L, H, Dh)
    return h, C, n, m


def _mlstm(p, gate_b, norm_g, C0, n0, m0):
    B, L, _ = p.shape
    pf = p.astype(F32)
    q, k, v, o, g = jnp.split(pf, ML_SPLITS, axis=-1)
    hd = lambda t: t.reshape(B, L, ML_HEADS, HEAD_DIM)
    q, k, v = hd(q) * (HEAD_DIM ** -0.5), hd(k), hd(v)
    g = (g + gate_b.astype(F32)).reshape(B, L, 2, 2, ML_HEADS)
    log_i = g[:, :, :, 0]
    log_f = jax.nn.log_sigmoid(g[:, :, :, 1])
    h_f, Cf, nf, mf = _mlstm_scan(q, k, v, log_i[:, :, 0], log_f[:, :, 0],
                                  C0[:, 0].astype(F32), n0[:, 0].astype(F32), m0[:, 0].astype(F32))
    fl = lambda t: jnp.flip(t, axis=1)
    h_b, Cb, nb, mb = _mlstm_scan(fl(q), fl(k), fl(v), fl(log_i[:, :, 1]), fl(log_f[:, :, 1]),
                                  C0[:, 1].astype(F32), n0[:, 1].astype(F32), m0[:, 1].astype(F32))
    h = _head_norm(h_f + fl(h_b), 1e-6).reshape(B, L, GROUP_W) * norm_g.astype(F32)
    out = (h * jax.nn.sigmoid(o)).astype(p.dtype)
    return out, jnp.stack([Cf, Cb], 1), jnp.stack([nf, nb], 1), jnp.stack([mf, mb], 1)


def _rwkv_scan(r, w, k, v, a_vec, b_vec, S0, reverse):
    def step(S, inp):
        r_t, w_t, k_t, v_t, a_t, b_t = inp
        sa = jnp.einsum('bhvk,bhk->bhv', S, a_t)
        S = S * w_t[:, :, None, :] + sa[..., None] * b_t[:, :, None, :] + v_t[..., :, None] * k_t[:, :, None, :]
        return S, jnp.einsum('bhvk,bhk->bhv', S, r_t)

    xs = tuple(jnp.moveaxis(t, 1, 0) for t in (r, w, k, v, a_vec, b_vec))
    S, ys = lax.scan(step, S0, xs, reverse=reverse)
    return S, jnp.moveaxis(ys, 0, 1)


def _rwkv(p, lp, S0):
    B, L, _ = p.shape
    f = lambda name: lp[name].astype(F32)
    pf = p.astype(F32)
    pf = pf + f('rw_mu') * (0.5 * (_prev(pf) + _next(pf)) - pf)
    r, k, v, lw, la, lg = jnp.split(pf, RW_SPLITS, axis=-1)
    hd = lambda t: t.reshape(B, L, RW_HEADS, HEAD_DIM)
    g = jax.nn.sigmoid(lg) @ f('rw_g2')
    kk = hd(k * f('rw_kk'))
    kk = kk / jnp.maximum(jnp.linalg.norm(kk, axis=-1, keepdims=True), 1e-12)
    tw = jnp.tanh(lw)
    rh, vh = hd(r), hd(v)
    rk = f('rw_rk').reshape(RW_HEADS, HEAD_DIM)
    scans, bonus, states = [], [], []
    for d in range(2):
        w = jnp.exp(-RW_DECAY_SCALE * jax.nn.sigmoid(f('rw_w0')[d] + tw @ f('rw_w2')[d]))
        a = jax.nn.sigmoid(f('rw_a0')[d] + la @ f('rw_a2')[d])
        kt = hd(k * (1.0 + (a - 1.0) * f('rw_ka')))
        S, y = _rwkv_scan(rh, hd(w), kt, vh, -kk, kk * hd(a), S0[:, d].astype(F32), d == 1)
        scans.append(y)
        bonus.append(jnp.sum(rh * kt * rk, axis=-1, keepdims=True) * vh)
        states.append(S)
    y = _head_norm(scans[0] + scans[1], RW_GN_EPS).reshape(B, L, GROUP_W) * f('rw_ln_g') + f('rw_ln_b')
    y = y + (bonus[0] + bonus[1]).reshape(B, L, GROUP_W)
    return (y * g).astype(p.dtype), jnp.stack(states, axis=1)


def _rope_2d(x):
    L = x.shape[1]
    rows = L // GRID_W
    pos_r = jnp.repeat(jnp.arange(rows, dtype=F32), GRID_W)
    pos_c = jnp.tile(jnp.arange(GRID_W, dtype=F32), rows)
    inv = ROPE_THETA ** (-jnp.arange(ROPE_FREQS, dtype=F32) / ROPE_FREQS)

    def rot(xa, pos):
        ang = pos[:, None] * inv[None, :]
        cos = jnp.cos(ang)[None, :, None, :]
        sin = jnp.sin(ang)[None, :, None, :]
        x1, x2 = xa[..., :ROPE_FREQS], xa[..., ROPE_FREQS:]
        return jnp.concatenate([x1 * cos - x2 * sin, x1 * sin + x2 * cos], -1)

    xf = x.astype(F32)
    return jnp.concatenate([rot(xf[..., :ROPE_AXIS], pos_r), rot(xf[..., ROPE_AXIS:], pos_c)], -1).astype(x.dtype)


def _attend_blocks(q, k, v):
    B, Lq, H, Dh = q.shape
    kvh = k.shape[2]
    G = H // kvh
    qb = jnp.moveaxis(q.reshape(B, Lq // AT_BLOCK, AT_BLOCK, kvh, G, Dh), 1, 0)

    def one(qblk):
        s = jnp.einsum('btkgd,bskd->bkgts', qblk, k).astype(F32) * (Dh ** -0.5)
        prob = jax.nn.softmax(s, axis=-1)
        return jnp.einsum('bkgts,bskd->btkgd', prob.astype(v.dtype), v)

    o = lax.map(one, qb)
    return jnp.moveaxis(o, 0, 1).reshape(B, Lq, H * Dh)


def _attn(p, qn, kn, ctx_kv):
    B, L, _ = p.shape
    q, k, v = jnp.split(p, AT_SPLITS, axis=-1)
    q = _rmsnorm(q.reshape(B, L, AT_HEADS, HEAD_DIM), qn)
    k = _rmsnorm(k.reshape(B, L, AT_KV_HEADS, HEAD_DIM), kn)
    v = v.reshape(B, L, AT_KV_HEADS, HEAD_DIM)
    if ctx_kv is None:
        return _attend_blocks(q, k, v).astype(p.dtype), k, v
    q, k = _rope_2d(q), _rope_2d(k)
    kc, vc = ctx_kv
    keys = jnp.concatenate([k, kc.astype(k.dtype)], axis=1)
    vals = jnp.concatenate([v, vc.astype(v.dtype)], axis=1)
    return _attend_blocks(q, keys, vals).astype(p.dtype), k, v


def _layer(x, mod, lp, ctx):
    B, L, _ = x.shape
    sh_a, sc_a, g_a, sh_m, sc_m, g_m = jnp.split(mod[:, None, :].astype(x.dtype), 6, axis=-1)
    h = x * (1 + sc_a) + sh_a
    p = h @ lp['w_in']
    p_hy, p_ml, p_rw, p_at = jnp.split(p, IN_SPLITS, axis=-1)
    filt = _hyena_filters(L, lp['hy_w1'], lp['hy_b1'], lp['hy_freq'], lp['hy_w2'], lp['hy_b2'], lp['hy_w3'])
    y_hy = _hyena(p_hy, lp['hy_conv'], filt, lp['hy_bias'])
    if ctx is None:
        C0 = jnp.zeros((B, 2, ML_HEADS, HEAD_DIM, HEAD_DIM), F32)
        n0 = jnp.zeros((B, 2, ML_HEADS, HEAD_DIM), F32)
        m0 = jnp.zeros((B, 2, ML_HEADS), F32)
        S0 = jnp.zeros((B, 2, RW_HEADS, HEAD_DIM, HEAD_DIM), F32)
        kv = None
    else:
        kc, vc, C0, n0, m0, S0 = ctx
        kv = (kc, vc)
    y_ml, C, n, m = _mlstm(p_ml, lp['ml_gate_b'], lp['ml_norm_g'], C0, n0, m0)
    y_rw, S = _rwkv(p_rw, lp, S0)
    y_at, k, v = _attn(p_at, lp['at_qn'], lp['at_kn'], kv)
    mix = jnp.concatenate([y_hy, y_ml, y_rw, y_at], axis=-1) @ lp['w_out']
    x = _layernorm(ALPHA * x + g_a * mix, lp['ln1_g'], lp['ln1_b'])
    h = x * (1 + sc_m) + sh_m
    ff = jnp.square(jax.nn.relu(h @ lp['mlp_w1'])) @ lp['mlp_w2']
    x = _layernorm(ALPHA * x + g_m * ff, lp['ln2_g'], lp['ln2_b'])
    return x, (k, v, C, n, m, S)


def setup_inputs(seed: int = 0) -> dict:
    key = jax.random.key(seed)
    ks = iter(jax.random.split(key, 64))
    nrm = lambda shape, scale: scale * jax.random.normal(next(ks), shape, F32)
    D = D_MODEL
    W = GROUP_W
    gate_off = jnp.concatenate([jnp.zeros(2 * D), jnp.ones(D), jnp.zeros(2 * D), jnp.ones(D)]).astype(F32)
    ml_off = jnp.concatenate([jnp.zeros(ML_HEADS), jnp.linspace(3.0, 6.0, ML_HEADS)] * 2).astype(F32)
    conv_off = jnp.array([0.0, 1.0, 0.0], F32)[:, None]
    return {
        'x_prompt': nrm((BATCH, SEQ, D), 1.0),
        'x_sample': nrm((DEC_BATCH, DEC_SEQ, D), 1.0),
        'cache_attn_k': nrm((DEC_BATCH, DEPTH, PAST_LEN, AT_KV_HEADS, HEAD_DIM), 1.0),
        'cache_attn_v': nrm((DEC_BATCH, DEPTH, PAST_LEN, AT_KV_HEADS, HEAD_DIM), 1.0),
        'state_mlstm_C': nrm((DEC_BATCH, DEPTH, 2, ML_HEADS, HEAD_DIM, HEAD_DIM), 0.3),
        'state_mlstm_n': nrm((DEC_BATCH, DEPTH, 2, ML_HEADS, HEAD_DIM), 0.3),
        'state_mlstm_m': nrm((DEC_BATCH, DEPTH, 2, ML_HEADS), 0.5),
        'state_rwkv_S': nrm((DEC_BATCH, DEPTH, 2, RW_HEADS, HEAD_DIM, HEAD_DIM), 0.3),
        'c': nrm((DEC_BATCH, D), 1.0),
        'c_ctx': nrm((D,), 1.0),
        'w_mod': nrm((DEPTH, D, 6 * D), 0.1 * D ** -0.5),
        'b_mod': nrm((DEPTH, 6 * D), 0.02) + gate_off,
        'w_in': nrm((DEPTH, D, P_IN), D ** -0.5),
        'hy_conv': nrm((DEPTH, 3, HY_COLS), 0.3) + conv_off,
        'hy_w1': nrm((DEPTH, HY_EMB, HY_FFN), HY_EMB ** -0.5),
        'hy_b1': nrm((DEPTH, HY_FFN), 0.1),
        'hy_freq': 1.0 + nrm((DEPTH, HY_FFN), 0.02),
        'hy_w2': nrm((DEPTH, HY_FFN, HY_FFN), HY_FFN ** -0.5),
        'hy_b2': nrm((DEPTH, HY_FFN), 0.1),
        'hy_w3': nrm((DEPTH, HY_FFN, 4 * HY_W), HY_FFN ** -0.5),
        'hy_bias': nrm((DEPTH, 2, HY_W), 0.1),
        'ml_gate_b': nrm((DEPTH, 4 * ML_HEADS), 0.1) + ml_off,
        'ml_norm_g': 1.0 + nrm((DEPTH, W), 0.02),
        'rw_mu': 0.5 + nrm((DEPTH, RW_COLS), 0.1),
        'rw_w0': nrm((DEPTH, 2, W), 0.5),
        'rw_w2': nrm((DEPTH, 2, RW_W_RANK, W), 0.1 * RW_W_RANK ** -0.5),
        'rw_a0': nrm((DEPTH, 2, W), 0.5),
        'rw_a2': nrm((DEPTH, 2, RW_A_RANK, W), 0.1 * RW_A_RANK ** -0.5),
        'rw_g2': nrm((DEPTH, RW_G_RANK, W), RW_G_RANK ** -0.5),
        'rw_kk': 1.0 + nrm((DEPTH, W), 0.1),
        'rw_ka': 1.0 + nrm((DEPTH, W), 0.1),
        'rw_rk': nrm((DEPTH, W), 0.1),
        'rw_ln_g': 1.0 + nrm((DEPTH, W), 0.02),
        'rw_ln_b': nrm((DEPTH, W), 0.02),
        'at_qn': 1.0 + nrm((DEPTH, HEAD_DIM), 0.02),
        'at_kn': 1.0 + nrm((DEPTH, HEAD_DIM), 0.02),
        'w_out': nrm((DEPTH, D, D), BETA * D ** -0.5),
        'ln1_g': 1.0 + nrm((DEPTH, D), 0.02),
        'ln1_b': nrm((DEPTH, D), 0.02),
        'mlp_w1': nrm((DEPTH, D, D_FF), D ** -0.5),
        'mlp_w2': nrm((DEPTH, D_FF, D), BETA * D_FF ** -0.5),
        'ln2_g': 1.0 + nrm((DEPTH, D), 0.02),
        'ln2_b': nrm((DEPTH, D), 0.02),
    }


def reference(x_prompt, x_sample, cache_attn_k, cache_attn_v, state_mlstm_C, state_mlstm_n, state_mlstm_m,
              state_rwkv_S, c, c_ctx, w_mod, b_mod, w_in, hy_conv, hy_w1, hy_b1, hy_freq, hy_w2, hy_b2, hy_w3,
              hy_bias, ml_gate_b, ml_norm_g, rw_mu, rw_w0, rw_w2, rw_a0, rw_a2, rw_g2, rw_kk, rw_ka, rw_rk,
              rw_ln_g, rw_ln_b, at_qn, at_kn, w_out, ln1_g, ln1_b, mlp_w1, mlp_w2, ln2_g, ln2_b):
    params = {
        'w_in': w_in, 'hy_conv': hy_conv, 'hy_w1': hy_w1, 'hy_b1': hy_b1, 'hy_freq': hy_freq,
        'hy_w2': hy_w2, 'hy_b2': hy_b2, 'hy_w3': hy_w3, 'hy_bias': hy_bias,
        'ml_gate_b': ml_gate_b, 'ml_norm_g': ml_norm_g,
        'rw_mu': rw_mu, 'rw_w0': rw_w0, 'rw_w2': rw_w2, 'rw_a0': rw_a0, 'rw_a2': rw_a2, 'rw_g2': rw_g2,
        'rw_kk': rw_kk, 'rw_ka': rw_ka, 'rw_rk': rw_rk, 'rw_ln_g': rw_ln_g, 'rw_ln_b': rw_ln_b,
        'at_qn': at_qn, 'at_kn': at_kn, 'w_out': w_out, 'ln1_g': ln1_g, 'ln1_b': ln1_b,
        'mlp_w1': mlp_w1, 'mlp_w2': mlp_w2, 'ln2_g': ln2_g, 'ln2_b': ln2_b,
    }
    x = x_prompt
    ctx_states = []
    for l in range(DEPTH):
        lp = {name: arr[l] for name, arr in params.items()}
        mod = jax.nn.silu(c_ctx)[None, :] @ w_mod[l] + b_mod[l]
        x, st = _layer(x, mod, lp, None)
        ctx_states.append(st)
    y_prompt = x
    new_attn_k = jnp.stack([st[0] for st in ctx_states], axis=1)
    new_attn_v = jnp.stack([st[1] for st in ctx_states], axis=1)
    new_mlstm_C = jnp.stack([st[2] for st in ctx_states], axis=1)
    new_mlstm_n = jnp.stack([st[3] for st in ctx_states], axis=1)
    new_mlstm_m = jnp.stack([st[4] for st in ctx_states], axis=1)
    new_rwkv_S = jnp.stack([st[5] for st in ctx_states], axis=1)
    x = x_sample
    for l in range(DEPTH):
        lp = {name: arr[l] for name, arr in params.items()}
        mod = jax.nn.silu(c) @ w_mod[l] + b_mod[l]
        ctx = (cache_attn_k[:, l], cache_attn_v[:, l], state_mlstm_C[:, l], state_mlstm_n[:, l],
               state_mlstm_m[:, l], state_rwkv_S[:, l])
        x, _ = _layer(x, mod, lp, ctx)
    y_sample = x
    return (y_prompt, y_sample, new_attn_k, new_attn_v, new_mlstm_C, new_mlstm_n, new_mlstm_m, new_rwkv_S)
```

```python
import functools
import math

import jax
import jax.numpy as jnp
from jax import lax
from jax.experimental import pallas as pl
from jax.experimental.pallas import tpu as pltpu

F32 = jnp.float32
BF16 = jnp.bfloat16
HI = lax.Precision.HIGHEST

D = 1024
DEPTH = 4
W = 256
HD = 64
NH = 4
GRID_W = 64
D_FF = 4 * D
ALPHA = (2.0 * DEPTH) ** 0.25
HY_BANDS = 16
HY_EMB = 2 * HY_BANDS + 1
HY_FFN = 64
HY_MAX_DECAY = math.log(1e-2) / 0.3
HY_MIN_DECAY = math.log(1e-2) / 1.5
ML_CHUNK = 128
RW_CHUNK = 64
RW_DECAY_SCALE = 0.606531
RW_GN_EPS = 64e-5
ROPE_THETA = 10000.0

PC = 3456
C_HY = 0
C_ML = 768
C_RW = 1792
C_RWX = 2560
C_AT = 2816
C_MLG = 3328

VMEM_LIMIT = 56 * 1024 * 1024


def _cparams(sem):
    return pltpu.CompilerParams(dimension_semantics=sem, vmem_limit_bytes=VMEM_LIMIT)


def _mm(a, b, hi=True):
    if hi:
        return jnp.dot(a, b, preferred_element_type=F32, precision=HI)
    return jnp.dot(a.astype(BF16), b.astype(BF16), preferred_element_type=F32)


def _mm_nt(a, b, hi=True):
    dn = (((1,), (1,)), ((), ()))
    if hi:
        return lax.dot_general(a, b, dn, preferred_element_type=F32, precision=HI)
    return lax.dot_general(a.astype(BF16), b.astype(BF16), dn, preferred_element_type=F32)


def _head_ones(width):
    r = lax.broadcasted_iota(jnp.int32, (width, width), 0) >> 6
    c = lax.broadcasted_iota(jnp.int32, (width, width), 1) >> 6
    return (r == c).astype(F32)


def _layernorm(x, g, b):
    mu = jnp.mean(x, -1, keepdims=True)
    xc = x - mu
    var = jnp.mean(xc * xc, -1, keepdims=True)
    return xc * lax.rsqrt(var + 1e-5) * g + b


def _head_norm(x, eps):
    ones = _head_ones(x.shape[1])
    mu = _mm(x, ones) * (1.0 / HD)
    xc = x - mu
    var = _mm(xc * xc, ones) * (1.0 / HD)
    return xc * lax.rsqrt(var + eps)


def _mod_kernel(c_ref, w_ref, b_ref, o_ref):
    c = c_ref[...]
    s = c * jax.nn.sigmoid(c)
    o_ref[0] = _mm(s, w_ref[0]) + b_ref[0]


def _mod_all(cvec, w_mod, b_mod):
    tn = 1536
    return pl.pallas_call(
        _mod_kernel,
        grid=(DEPTH, 6 * D // tn),
        in_specs=[pl.BlockSpec((8, D), lambda l, n: (0, 0)),
                  pl.BlockSpec((1, D, tn), lambda l, n: (l, 0, n)),
                  pl.BlockSpec((1, 1, tn), lambda l, n: (l, 0, n))],
        out_specs=pl.BlockSpec((1, 8, tn), lambda l, n: (l, 0, n)),
        out_shape=jax.ShapeDtypeStruct((DEPTH, 8, 6 * D), F32),
        compiler_params=_cparams(("arbitrary", "arbitrary")),
        name="mod",
    )(cvec, w_mod, b_mod.reshape(DEPTH, 1, 6 * D))


def _mod_row_map(rows_per_cond, tm, base):
    return lambda i: (base + (i * tm) // rows_per_cond, 0, 0)


def _inproj_kernel(x_ref, mod_ref, w_ref, o_ref):
    m = mod_ref[0]
    h = x_ref[...] * (1.0 + m[:, D:2 * D]) + m[:, 0:D]
    o_ref[...] = jnp.dot(h.astype(BF16), w_ref[...], preferred_element_type=F32)


def _inproj(x, mod8, w_in_p, rows_per_cond, base):
    n = x.shape[0]
    tm, tn = 512, 1152
    rmap = _mod_row_map(rows_per_cond, tm, base)
    return pl.pallas_call(
        _inproj_kernel,
        grid=(PC // tn, n // tm),
        in_specs=[pl.BlockSpec((tm, D), lambda c, i: (i, 0)),
                  pl.BlockSpec((1, 1, 6 * D), lambda c, i: rmap(i)),
                  pl.BlockSpec((D, tn), lambda c, i: (0, c))],
        out_specs=pl.BlockSpec((tm, tn), lambda c, i: (i, c)),
        out_shape=jax.ShapeDtypeStruct((n, PC), F32),
        compiler_params=_cparams(("arbitrary", "arbitrary")),
        name="inproj",
    )(x, mod8, w_in_p)


def _outmlp_kernel(yh, ym, yr, ya, x_ref, mod_ref, wo, g1, b1, w1, w2, g2, b2, o_ref):
    m = mod_ref[0]
    y = jnp.concatenate([yh[...], ym[...], yr[...], ya[...]], axis=-1).astype(BF16)
    mix = jnp.dot(y, wo[...], preferred_element_type=F32)
    x1 = _layernorm(ALPHA * x_ref[...] + m[:, 2 * D:3 * D] * mix, g1[...], b1[...])
    h = (x1 * (1.0 + m[:, 4 * D:5 * D]) + m[:, 3 * D:4 * D]).astype(BF16)
    acc = jnp.zeros(x1.shape, F32)
    for c in range(D_FF // D):
        a = jnp.dot(h, w1[:, c * D:(c + 1) * D], preferred_element_type=F32)
        a = jnp.square(jnp.maximum(a, 0.0)).astype(BF16)
        acc = acc + jnp.dot(a, w2[c * D:(c + 1) * D, :], preferred_element_type=F32)
    o_ref[...] = _layernorm(ALPHA * x1 + m[:, 5 * D:6 * D] * acc, g2[...], b2[...])


def _outmlp(ys, x, mod8, wo, g1, b1, w1, w2, g2, b2, rows_per_cond, base):
    n = x.shape[0]
    tm = 512
    rmap = _mod_row_map(rows_per_cond, tm, base)
    const = lambda shape: pl.BlockSpec(shape, lambda i: (0, 0), pipeline_mode=pl.Buffered(1))
    row = lambda wd: pl.BlockSpec((tm, wd), lambda i: (i, 0))
    return pl.pallas_call(
        _outmlp_kernel,
        grid=(n // tm,),
        in_specs=[row(W), row(W), row(W), row(W), row(D),
                  pl.BlockSpec((1, 1, 6 * D), lambda i: rmap(i)),
                  const((D, D)), const((1, D)), const((1, D)),
                  const((D, D_FF)), const((D_FF, D)), const((1, D)), const((1, D))],
        out_specs=row(D),
        out_shape=jax.ShapeDtypeStruct((n, D), F32),
        compiler_params=_cparams(("arbitrary",)),
        name="outmlp",
    )(*ys, x, mod8, wo, g1, b1, w1, w2, g2, b2)


def _hyfilt_kernel(z_ref, w1, b1, fr, w2, b2, w3, dec_ref, h_ref, nrm_ref):
    i = pl.program_id(0)
    f = fr[...]
    hid = jnp.sin(f * (_mm(z_ref[...], w1[...]) + b1[...]))
    hid = jnp.sin(f * (_mm(hid, w2[...]) + b2[...]))
    h = _mm(hid, w3[...])
    dec = dec_ref[...]
    h = h * jnp.concatenate([dec, dec, dec, dec], axis=1)
    rid = lax.broadcasted_iota(jnp.int32, h.shape, 0)
    cid = lax.broadcasted_iota(jnp.int32, h.shape, 1)
    anti = ((cid >> 8) & 1) == 1
    h = jnp.where(anti & (rid == 0) & (i == 0), 0.0, h)
    h_ref[...] = h.astype(BF16)
    part = jnp.broadcast_to(jnp.sum(jnp.abs(h), axis=0, keepdims=True), (8, 4 * W))

    @pl.when(i == 0)
    def _():
        nrm_ref[...] = part

    @pl.when(i > 0)
    def _():
        nrm_ref[...] = nrm_ref[...] + part


def _hyfilt(L, z, w1p, b1, fr, w2, b2, w3, dec):
    tl = min(L, 512)
    const = lambda shape: pl.BlockSpec(shape, lambda i: (0, 0))
    return pl.pallas_call(
        _hyfilt_kernel,
        grid=(L // tl,),
        in_specs=[pl.BlockSpec((tl, 128), lambda i: (i, 0)),
                  const((128, HY_FFN)), const((1, HY_FFN)), const((1, HY_FFN)),
                  const((HY_FFN, HY_FFN)), const((1, HY_FFN)), const((HY_FFN, 4 * W)),
                  pl.BlockSpec((tl, W), lambda i: (i, 0))],
        out_specs=[pl.BlockSpec((tl, 4 * W), lambda i: (i, 0)), const((8, 4 * W))],
        out_shape=[jax.ShapeDtypeStruct((L, 4 * W), BF16), jax.ShapeDtypeStruct((8, 4 * W), F32)],
        compiler_params=_cparams(("arbitrary",)),
        name="hyfilt",
    )(z, w1p, b1, fr, w2, b2, w3, dec)


def _hyspec_kernel(fc, fs, h_ref, nrm_ref, re_ref, im_ref, *, L):
    k = pl.program_id(0)
    h = h_ref[...]
    re = jnp.dot(fc[...], h, preferred_element_type=F32)
    im = jnp.dot(fs[...], h, preferred_element_type=F32)
    tk = re.shape[0]
    rid = lax.broadcasted_iota(jnp.int32, (tk, W), 0) + k * tk
    packed = rid == 0
    scale = jnp.where(packed, 1.0, 2.0) * (1.0 / (2 * L))
    nrm = nrm_ref[0:1, :]
    res, ims = [], []
    for f in range(2):
        c0, c1 = 2 * f * W, (2 * f + 1) * W
        inv = scale / (nrm[:, c0:c0 + W] + nrm[:, c1:c1 + W])
        res.append((re[:, c0:c0 + W] + re[:, c1:c1 + W]) * inv)
        ims.append(jnp.where(packed, im[:, c0:c0 + W] + im[:, c1:c1 + W],
                             im[:, c0:c0 + W] - im[:, c1:c1 + W]) * inv)
    re_ref[...] = jnp.concatenate(res, axis=1)
    im_ref[...] = jnp.concatenate(ims, axis=1)


def _hyspec(L, fc, fs, h, nrm):
    tk = min(L, 512)
    return pl.pallas_call(
        functools.partial(_hyspec_kernel, L=L),
        grid=(L // tk,),
        in_specs=[pl.BlockSpec((tk, L), lambda k: (k, 0)),
                  pl.BlockSpec((tk, L), lambda k: (k, 0)),
                  pl.BlockSpec((L, 4 * W), lambda k: (0, 0), pipeline_mode=pl.Buffered(1)),
                  pl.BlockSpec((8, 4 * W), lambda k: (0, 0))],
        out_specs=[pl.BlockSpec((tk, 2 * W), lambda k: (k, 0))] * 2,
        out_shape=[jax.ShapeDtypeStruct((L, 2 * W), F32)] * 2,
        compiler_params=_cparams(("arbitrary",)),
        name="hyspec",
    )(fc, fs, h, nrm)


def _hypre_kernel(x_ref, w_ref, o_ref):
    x = x_ref[...]
    L = x.shape[0]
    w = w_ref[0]
    rid = lax.broadcasted_iota(jnp.int32, x.shape, 0)
    xp = jnp.where(rid == 0, 0.0, pltpu.roll(x, 1, axis=0))
    xn = jnp.where(rid == L - 1, 0.0, pltpu.roll(x, L - 1, axis=0))
    o_ref[...] = xp * w[0:1, :] + x * w[1:2, :] + xn * w[2:3, :]


def _hypre(p, B, L, conv_w3):
    return pl.pallas_call(
        _hypre_kernel,
        grid=(B, 3),
        in_specs=[pl.BlockSpec((L, W), lambda b, c: (b, c)),
                  pl.BlockSpec((1, 3, W), lambda b, c: (c, 0, 0))],
        out_specs=pl.BlockSpec((L, W), lambda b, c: (b, c)),
        out_shape=jax.ShapeDtypeStruct((B * L, 3 * W), F32),
        compiler_params=_cparams(("arbitrary", "arbitrary")),
        name="hypre",
    )(p, conv_w3)


def _hyfwd_kernel(u_ref, fc, fs, fre_ref, fim_ref, yre_ref, yim_ref, ub):
    k = pl.program_id(1)

    @pl.when(k == 0)
    def _():
        ub[...] = u_ref[...].astype(BF16)

    u = ub[...]
    re = jnp.dot(fc[...], u, preferred_element_type=F32)
    im = jnp.dot(fs[...], u, preferred_element_type=F32)
    fre, fim = fre_ref[...], fim_ref[...]
    tk = re.shape[0]
    packed = (lax.broadcasted_iota(jnp.int32, re.shape, 0) + k * tk) == 0
    yre_ref[...] = jnp.where(packed, re * fre, re * fre - im * fim)
    yim_ref[...] = jnp.where(packed, im * fim, re * fim + im * fre)


def _hyfwd(u, ucol, B, L, fc, fs, fre, fim, f):
    tk = min(L, 512)
    return pl.pallas_call(
        _hyfwd_kernel,
        grid=(B, L // tk),
        in_specs=[pl.BlockSpec((L, W), lambda b, k: (b, ucol)),
                  pl.BlockSpec((tk, L), lambda b, k: (k, 0)),
                  pl.BlockSpec((tk, L), lambda b, k: (k, 0)),
                  pl.BlockSpec((tk, W), lambda b, k: (k, f)),
                  pl.BlockSpec((tk, W), lambda b, k: (k, f))],
        out_specs=[pl.BlockSpec((tk, W), lambda b, k: (b * (L // tk) + k, 0))] * 2,
        out_shape=[jax.ShapeDtypeStruct((B * L, W), F32)] * 2,
        scratch_shapes=[pltpu.VMEM((L, W), BF16)],
        compiler_params=_cparams(("arbitrary", "arbitrary")),
        name="hyfwd",
    )(u, fc, fs, fre, fim)


def _hyinv_kernel(yre_ref, yim_ref, fc, fst, u_ref, g_ref, bias_ref, o_ref, yb):
    t = pl.program_id(1)

    @pl.when(t == 0)
    def _():
        yb[0] = yre_ref[...].astype(BF16)
        yb[1] = yim_ref[...].astype(BF16)

    y = jnp.dot(fc[...], yb[0], preferred_element_type=F32) + jnp.dot(fst[...], yb[1], preferred_element_type=F32)
    o_ref[...] = g_ref[...] * (y + u_ref[...] * bias_ref[0])


def _hyinv(yre, yim, B, L, fc, fst, u, ucol, gate, gcol, bias, f):
    tt = min(L, 512)
    nt = L // tt
    return pl.pallas_call(
        _hyinv_kernel,
        grid=(B, nt),
        in_specs=[pl.BlockSpec((L, W), lambda b, t: (b, 0)),
                  pl.BlockSpec((L, W), lambda b, t: (b, 0)),
                  pl.BlockSpec((tt, L), lambda b, t: (t, 0)),
                  pl.BlockSpec((tt, L), lambda b, t: (t, 0)),
                  pl.BlockSpec((tt, W), lambda b, t: (b * nt + t, ucol)),
                  pl.BlockSpec((tt, W), lambda b, t: (b * nt + t, gcol)),
                  pl.BlockSpec((1, 1, W), lambda b, t: (f, 0, 0))],
        out_specs=pl.BlockSpec((tt, W), lambda b, t: (b * nt + t, 0)),
        out_shape=jax.ShapeDtypeStruct((B * L, W), F32),
        scratch_shapes=[pltpu.VMEM((2, L, W), BF16)],
        compiler_params=_cparams(("arbitrary", "arbitrary")),
        name="hyinv",
    )(yre, yim, fc, fst, u, gate, bias)


def _dft_consts(L):
    s = int(round(math.sqrt(L)))
    assert s * s == L
    k = jnp.arange(L, dtype=jnp.int32)[:, None]
    n1 = jnp.arange(s, dtype=jnp.int32)[None, :]
    a1 = ((k * (n1 * s)) % (2 * L)).astype(F32) * (math.pi / L)
    a2 = ((k * n1) % (2 * L)).astype(F32) * (math.pi / L)
    c1, s1, c2, s2 = jnp.cos(a1), jnp.sin(a1), jnp.cos(a2), jnp.sin(a2)
    fc = (c1[:, :, None] * c2[:, None, :] - s1[:, :, None] * s2[:, None, :]).reshape(L, L)
    fs = -(s1[:, :, None] * c2[:, None, :] + c1[:, :, None] * s2[:, None, :]).reshape(L, L)
    nyq = (1.0 - 2.0 * (jnp.arange(L) % 2)).astype(F32)[None, :]
    fs = jnp.where(k == 0, nyq, fs)
    return fc.astype(BF16), fs.astype(BF16), fs.T.astype(BF16)


def _hyena_consts(L):
    t = jnp.linspace(0.0, 1.0, L, dtype=F32)[:, None]
    ang = (2.0 * math.pi / L) * jnp.arange(L, dtype=F32)[:, None]
    bands = jnp.linspace(1e-4, HY_BANDS - 1, HY_BANDS, dtype=F32)[None, :]
    z = jnp.concatenate([t, jnp.cos(bands * ang), jnp.sin(bands * ang)], axis=-1)
    z = jnp.pad(z, ((0, 0), (0, 128 - HY_EMB)))
    deltas = jnp.abs(jnp.linspace(HY_MIN_DECAY, HY_MAX_DECAY, W, dtype=F32))
    dec = jnp.exp(-t * deltas)
    return z, dec


def _hyena(p, B, L, hc, lp):
    z, dec, fc, fs, fst = hc
    w1p = jnp.pad(lp['hy_w1'], ((0, 128 - HY_EMB), (0, 0)))
    row = lambda a: a.reshape(1, -1)
    h, nrm = _hyfilt(L, z, w1p, row(lp['hy_b1']), row(lp['hy_freq']), lp['hy_w2'], row(lp['hy_b2']),
                     lp['hy_w3'], dec)
    fre, fim = _hyspec(L, fc, fs, h, nrm)
    pc = _hypre(p, B, L, lp['hy_conv'].reshape(3, 3, W).transpose(1, 0, 2))
    bias = lp['hy_bias'].reshape(2, 1, W)
    yre, yim = _hyfwd(pc, 0, B, L, fc, fs, fre, fim, 0)
    zmid = _hyinv(yre, yim, B, L, fc, fst, pc, 0, pc, 1, bias, 0)
    yre, yim = _hyfwd(zmid, 0, B, L, fc, fs, fre, fim, 1)
    return _hyinv(yre, yim, B, L, fc, fst, zmid, 0, pc, 2, bias, 1)


def _mlstm_kernel(qf, kf, vf, gf, qb, kb, vb, gb, gbias, c0, n0, m0,
                  hf_o, hb_o, c_o, n_o, m_o, c_s, n_s, m_s, *, nc):
    j = pl.program_id(1)
    T = ML_CHUNK

    @pl.when(j == 0)
    def _():
        c_s[...] = c0[0]
        n_s[...] = n0[0]
        m_s[...] = m0[0]

    lane = lax.broadcasted_iota(jnp.int32, (T, 128), 1)
    is_f = ((lane & 4) != 0) & (lane < 16)
    row = lax.broadcasted_iota(jnp.int32, (T, T), 0)
    col = lax.broadcasted_iota(jnp.int32, (T, T), 1)
    for d, (q_ref, k_ref, v_ref, g_ref, h_o) in enumerate(((qf, kf, vf, gf, hf_o), (qb, kb, vb, gb, hb_o))):
        g = g_ref[...] + gbias[...]
        G = jnp.where(is_f, jnp.minimum(g, 0.0) - jnp.log1p(jnp.exp(-jnp.abs(g))), g)
        mask = (row >= col) if d == 0 else (row <= col)
        Bc = _mm(mask.astype(F32), G)
        GT = G.T
        BcT = Bc.T
        q = q_ref[...] * (HD ** -0.5)
        k = k_ref[...]
        v = v_ref[...]
        hs = []
        for h in range(NH):
            u = d * NH + h
            cf, ci = d * 8 + 4 + h, d * 8 + h
            sl = slice(h * HD, (h + 1) * HD)
            qh, kh, vh = q[:, sl], k[:, sl], v[:, sl]
            b_col, b_row = Bc[:, cf:cf + 1], BcT[cf:cf + 1, :]
            i_col, i_row = G[:, ci:ci + 1], GT[ci:ci + 1, :]
            b_tot = Bc[T - 1:T, cf:cf + 1] if d == 0 else Bc[0:1, cf:cf + 1]
            m_prev = m_s[u:u + 1, 0:1]
            C = c_s[u]
            n_row = n_s[u:u + 1, :]
            logd = jnp.where(mask, b_col - b_row + i_row, -jnp.inf)
            inter = b_col + m_prev
            m_t = jnp.maximum(inter, jnp.max(logd, axis=-1, keepdims=True))
            s = _mm_nt(qh, kh) * jnp.exp(logd - m_t)
            w_in = jnp.exp(inter - m_t)
            num = _mm(s, vh) + w_in * _mm(qh, C)
            den = jnp.sum(s, axis=-1, keepdims=True) + w_in * jnp.sum(qh * n_row, axis=-1, keepdims=True)
            hs.append(num / jnp.maximum(jnp.abs(den), jnp.exp(-m_t)))
            logw_col = b_tot - b_col + i_col
            m_new = jnp.maximum(b_tot + m_prev, jnp.max(logw_col, axis=0, keepdims=True))
            kw = kh * jnp.exp(logw_col - m_new)
            decay = jnp.exp(b_tot + m_prev - m_new)
            c_s[u] = decay * C + _mm(kw.T, vh)
            n_s[u:u + 1, :] = decay * n_row + jnp.sum(kw, axis=0, keepdims=True)
            m_s[u:u + 1, :] = jnp.broadcast_to(m_new, (1, 128))
        h_o[...] = jnp.concatenate(hs, axis=1)

    @pl.when(j == nc - 1)
    def _():
        c_o[0] = c_s[...]
        n_o[0] = n_s[...]
        m_o[0] = m_s[...]


def _mlstm_scan(p, B, L, gbias, c0, n0, m0):
    T = ML_CHUNK
    nc = L // T
    fwd = lambda cb: (lambda b, j: (b * nc + j, cb))
    bwd = lambda cb: (lambda b, j: (b * nc + nc - 1 - j, cb))
    blk = lambda f, cb, wd=W: pl.BlockSpec((T, wd), f(cb))
    cq, ck, cv, cg = C_ML // W, C_ML // W + 1, C_ML // W + 2, C_MLG // 128
    st = lambda shape: pl.BlockSpec((1,) + shape, lambda b, j: (b,) + (0,) * len(shape))
    return pl.pallas_call(
        functools.partial(_mlstm_kernel, nc=nc),
        grid=(B, nc),
        in_specs=[blk(fwd, cq), blk(fwd, ck), blk(fwd, cv), blk(fwd, cg, 128),
                  blk(bwd, cq), blk(bwd, ck), blk(bwd, cv), blk(bwd, cg, 128),
                  pl.BlockSpec((1, 128), lambda b, j: (0, 0)),
                  st((8, HD, HD)), st((8, HD)), st((8, 128))],
        out_specs=[pl.BlockSpec((T, W), lambda b, j: (b * nc + j, 0)),
                   pl.BlockSpec((T, W), lambda b, j: (b * nc + nc - 1 - j, 0)),
                   st((8, HD, HD)), st((8, HD)), st((8, 128))],
        out_shape=[jax.ShapeDtypeStruct((B * L, W), F32), jax.ShapeDtypeStruct((B * L, W), F32),
                   jax.ShapeDtypeStruct((B, 8, HD, HD), F32), jax.ShapeDtypeStruct((B, 8, HD), F32),
                   jax.ShapeDtypeStruct((B, 8, 128), F32)],
        scratch_shapes=[pltpu.VMEM((8, HD, HD), F32), pltpu.VMEM((8, HD), F32), pltpu.VMEM((8, 128), F32)],
        compiler_params=_cparams(("arbitrary", "arbitrary")),
        name="mlstm",
    )(p, p, p, p, p, p, p, p, gbias, c0, n0, m0)


def _mlfin_kernel(hf, hb, o_ref, g_ref, y_ref):
    h = _head_norm(hf[...] + hb[...], 1e-6) * g_ref[...]
    y_ref[...] = h * jax.nn.sigmoid(o_ref[...])


def _mlfin(hf, hb, p, norm_g):
    n = hf.shape[0]
    tm = 512
    row = lambda cb: pl.BlockSpec((tm, W), lambda i: (i, cb))
    return pl.pallas_call(
        _mlfin_kernel,
        grid=(n // tm,),
        in_specs=[row(0), row(0), row(C_ML // W + 3), pl.BlockSpec((1, W), lambda i: (0, 0))],
        out_specs=row(0),
        out_shape=jax.ShapeDtypeStruct((n, W), F32),
        compiler_params=_cparams(("arbitrary",)),
        name="mlfin",
    )(hf, hb, p, norm_g)


def _shifted(x, prev8, next8, first, last):
    T = x.shape[0]
    rid = lax.broadcasted_iota(jnp.int32, x.shape, 0)
    pr = jnp.where(first, 0.0, prev8[7:8, :])
    nx = jnp.where(last, 0.0, next8[0:1, :])
    xp = jnp.where(rid == 0, pr, pltpu.roll(x, 1, axis=0))
    xn = jnp.where(rid == T - 1, nx, pltpu.roll(x, T - 1, axis=0))
    return xp, xn


def _rwprep_kernel(*refs, L, T):
    xs, prevs, nexts = refs[0:4], refs[4:8], refs[8:12]
    mu, w0, w2, a0, a2, g2, kkw, kaw, rkw = refs[12:21]
    r_o, v_o, kk_o, g_o, bon_o, lw0_o, lw1_o, kt0_o, kt1_o, ba0_o, ba1_o = refs[21:]
    i = pl.program_id(0)
    first = (i * T) % L == 0
    last = ((i + 1) * T) % L == 0
    pf = []
    for c in range(4):
        x = xs[c][...]
        xp, xn = _shifted(x, prevs[c][...], nexts[c][...], first, last)
        pf.append(x + mu[:, c * W:(c + 1) * W] * (0.5 * (xp + xn) - x))
    r, k, v, misc = pf
    lw, la, lg = misc[:, 0:64], misc[:, 64:128], misc[:, 128:256]
    ones = _head_ones(W)
    g_o[...] = _mm(jax.nn.sigmoid(lg), g2[...])
    kq = k * kkw[...]
    kk = kq / jnp.maximum(jnp.sqrt(_mm(kq * kq, ones)), 1e-12)
    tw = jnp.tanh(lw)
    r_o[...] = r
    v_o[...] = v
    kk_o[...] = kk
    bon = jnp.zeros_like(r)
    for d, (lw_o, kt_o, ba_o) in enumerate(((lw0_o, kt0_o, ba0_o), (lw1_o, kt1_o, ba1_o))):
        lw_o[...] = -RW_DECAY_SCALE * jax.nn.sigmoid(w0[d:d + 1, :] + _mm(tw, w2[d]))
        a = jax.nn.sigmoid(a0[d:d + 1, :] + _mm(la, a2[d]))
        kt = k * (1.0 + (a - 1.0) * kaw[...])
        kt_o[...] = kt
        ba_o[...] = kk * a
        bon = bon + _mm(r * kt * rkw[...], ones) * v
    bon_o[...] = bon


def _rwprep(p, B, L, lp):
    n = B * L
    T = min(L, 512)
    nb8 = n // 8
    c0 = C_RW // W
    cur = lambda c: pl.BlockSpec((T, W), lambda i: (i, c0 + c))
    prv = lambda c: pl.BlockSpec((8, W), lambda i: (jnp.maximum(i * (T // 8) - 1, 0), c0 + c))
    nxt = lambda c: pl.BlockSpec((8, W), lambda i: (jnp.minimum((i + 1) * (T // 8), nb8 - 1), c0 + c))
    const = lambda shape: pl.BlockSpec(shape, lambda i: (0,) * len(shape))
    row = lambda a: a.reshape(1, -1)
    out = pl.BlockSpec((T, W), lambda i: (i, 0))
    return pl.pallas_call(
        functools.partial(_rwprep_kernel, L=L, T=T),
        grid=(n // T,),
        in_specs=[cur(c) for c in range(4)] + [prv(c) for c in range(4)] + [nxt(c) for c in range(4)]
        + [const((1, 4 * W)), const((2, W)), const((2, 64, W)), const((2, W)), const((2, 64, W)),
           const((128, W)), const((1, W)), const((1, W)), const((1, W))],
        out_specs=[out] * 11,
        out_shape=[jax.ShapeDtypeStruct((n, W), F32)] * 11,
        compiler_params=_cparams(("arbitrary",)),
        name="rwprep",
    )(*([p] * 12), row(lp['rw_mu']), lp['rw_w0'], lp['rw_w2'], lp['rw_a0'], lp['rw_a2'], lp['rw_g2'],
      row(lp['rw_kk']), row(lp['rw_ka']), row(lp['rw_rk']))


def _unit_tri_inverse(N, row, col):
    T = N.shape[0]
    X = (row == col).astype(F32)
    bs = 1
    while bs < T:
        sh = bs.bit_length() - 1
        pair = ((row >> (sh + 1)) == (col >> (sh + 1))) & ((row >> sh) != (col >> sh))
        No = jnp.where(pair, N, 0.0)
        X = X + _mm(_mm(X, No), X)
        bs *= 2
    return X


def _rwscan_kernel(rf, vf, kkf, lwf, ktf, baf, rb, vb, kkb, lwb, ktb, bab, s0,
                   yf_o, yb_o, s_o, s_s, *, nc):
    j = pl.program_id(1)
    T = RW_CHUNK

    @pl.when(j == 0)
    def _():
        s_s[...] = s0[0]

    row = lax.broadcasted_iota(jnp.int32, (T, T), 0)
    col = lax.broadcasted_iota(jnp.int32, (T, T), 1)
    dirs = ((rf, vf, kkf, lwf, ktf, baf, yf_o), (rb, vb, kkb, lwb, ktb, bab, yb_o))
    for d, (r_ref, v_ref, kk_ref, lw_ref, kt_ref, ba_ref, y_o) in enumerate(dirs):
        incl = (row >= col) if d == 0 else (row <= col)
        strict = (row > col) if d == 0 else (row < col)
        lw = lw_ref[...]
        lc = _mm(incl.astype(F32), lw)
        gam = jnp.exp(lc)
        ginv = jnp.exp(-lc)
        at = -kk_ref[...] * jnp.exp(lc - lw)
        bt = ba_ref[...] * ginv
        kt = kt_ref[...] * ginv
        rt = r_ref[...] * gam
        v = v_ref[...]
        gtot = gam[T - 1:T, :] if d == 0 else gam[0:1, :]
        ys = []
        for h in range(NH):
            u = d * NH + h
            sl = slice(h * HD, (h + 1) * HD)
            a_h, b_h, k_h, r_h, v_h = at[:, sl], bt[:, sl], kt[:, sl], rt[:, sl], v[:, sl]
            A_ab = jnp.where(strict, _mm_nt(a_h, b_h), 0.0)
            A_ak = jnp.where(strict, _mm_nt(a_h, k_h), 0.0)
            A_rb = jnp.where(incl, _mm_nt(r_h, b_h), 0.0)
            A_rk = jnp.where(incl, _mm_nt(r_h, k_h), 0.0)
            Tinv = _unit_tri_inverse(A_ab, row, col)
            S = s_s[u]
            U = _mm(Tinv, _mm_nt(a_h, S) + _mm(A_ak, v_h))
            ys.append(_mm_nt(r_h, S) + _mm(A_rb, U) + _mm(A_rk, v_h))
            s_s[u] = (S + _mm(U.T, b_h) + _mm(v_h.T, k_h)) * gtot[:, sl]
        y_o[...] = jnp.concatenate(ys, axis=1)

    @pl.when(j == nc - 1)
    def _():
        s_o[0] = s_s[...]


def _rwscan(prep, B, L, s0):
    r, v, kk, lw0, lw1, kt0, kt1, ba0, ba1 = prep
    T = RW_CHUNK
    nc = L // T
    fwd = pl.BlockSpec((T, W), lambda b, j: (b * nc + j, 0))
    bwd = pl.BlockSpec((T, W), lambda b, j: (b * nc + nc - 1 - j, 0))
    st = pl.BlockSpec((1, 8, HD, HD), lambda b, j: (b, 0, 0, 0))
    return pl.pallas_call(
        functools.partial(_rwscan_kernel, nc=nc),
        grid=(B, nc),
        in_specs=[fwd] * 6 + [bwd] * 6 + [st],
        out_specs=[fwd, bwd, st],
        out_shape=[jax.ShapeDtypeStruct((B * L, W), F32), jax.ShapeDtypeStruct((B * L, W), F32),
                   jax.ShapeDtypeStruct((B, 8, HD, HD), F32)],
        scratch_shapes=[pltpu.VMEM((8, HD, HD), F32)],
        compiler_params=_cparams(("arbitrary", "arbitrary")),
        name="rwscan",
    )(r, v, kk, lw0, kt0, ba0, r, v, kk, lw1, kt1, ba1, s0)


def _rwfin_kernel(yf, yb, bon, g, lng, lnb, y_ref):
    y = _head_norm(yf[...] + yb[...], RW_GN_EPS) * lng[...] + lnb[...] + bon[...]
    y_ref[...] = y * g[...]


def _rwfin(yf, yb, bon, g, lng, lnb):
    n = yf.shape[0]
    tm = 512
    row = pl.BlockSpec((tm, W), lambda i: (i, 0))
    const = pl.BlockSpec((1, W), lambda i: (0, 0))
    return pl.pallas_call(
        _rwfin_kernel,
        grid=(n // tm,),
        in_specs=[row, row, row, row, const, const],
        out_specs=row,
        out_shape=jax.ShapeDtypeStruct((n, W), F32),
        compiler_params=_cparams(("arbitrary",)),
        name="rwfin",
    )(yf, yb, bon, g, lng, lnb)


def _rms(x, gain):
    ms = _mm(x * x, _head_ones(x.shape[1])) * (1.0 / HD)
    return x * lax.rsqrt(ms + 1e-6) * gain


def _rope(x, cos, sin):
    wd = x.shape[1]
    lane = lax.broadcasted_iota(jnp.int32, x.shape, 1)
    partner = jnp.where((lane & 16) == 0, pltpu.roll(x, wd - 16, axis=1), pltpu.roll(x, 16, axis=1))
    return x * cos + partner * sin


def _attn_kernel(*refs, rope, cached, tq, kchunk):
    if rope:
        q_ref, k_ref, v_ref, qn, kn, cosq, sinq, cosk, sink = refs[:9]
        rest = refs[9:]
    else:
        q_ref, k_ref, v_ref, qn, kn = refs[:5]
        rest = refs[5:]
    if cached:
        kc_ref, vc_ref = rest[:2]
        rest = rest[2:]
    o_ref, kn_o, k_s = rest
    t = pl.program_id(1)

    @pl.when(t == 0)
    def _():
        k = _rms(k_ref[...], kn[...])
        if rope:
            k = _rope(k, cosk[...], sink[...])
        kn_o[...] = k
        k_s[...] = k.astype(BF16)

    q = _rms(q_ref[...], qn[...])
    if rope:
        q = _rope(q, cosq[...], sinq[...])
    q = (q * (HD ** -0.5)).astype(BF16)
    Lk = k_s.shape[0]
    for jkv in range(2):
        sl = slice(jkv * HD, (jkv + 1) * HD)
        q2 = jnp.concatenate([q[:, (2 * jkv) * HD:(2 * jkv + 1) * HD],
                              q[:, (2 * jkv + 1) * HD:(2 * jkv + 2) * HD]], axis=0)
        m = jnp.full((2 * tq, 1), -jnp.inf, F32)
        l = jnp.zeros((2 * tq, 1), F32)
        acc = jnp.zeros((2 * tq, HD), F32)
        chunks = [(k_s[c:c + kchunk, sl], v_ref[c:c + kchunk, sl].astype(BF16)) for c in range(0, Lk, kchunk)]
        if cached:
            chunks.append((kc_ref[0][:, sl].astype(BF16), vc_ref[0][:, sl].astype(BF16)))
        for kc, vc in chunks:
            s = _mm_nt(q2, kc, hi=False)
            m_new = jnp.maximum(m, jnp.max(s, axis=-1, keepdims=True))
            pr = jnp.exp(s - m_new)
            corr = jnp.exp(m - m_new)
            l = corr * l + jnp.sum(pr, axis=-1, keepdims=True)
            acc = corr * acc + jnp.dot(pr.astype(BF16), vc, preferred_element_type=F32)
            m = m_new
        o = acc / l
        o_ref[:, (2 * jkv) * HD:(2 * jkv + 1) * HD] = o[0:tq]
        o_ref[:, (2 * jkv + 1) * HD:(2 * jkv + 2) * HD] = o[tq:2 * tq]


def _attn(p, B, L, qn, kn, rope_tabs=None, cache=None):
    tq = min(L, 256)
    nt = L // tq
    kchunk = min(L, 1024)
    rope, cached = rope_tabs is not None, cache is not None
    cq, ck, cv = C_AT // W, (C_AT + W) // 128, (C_AT + W) // 128 + 1
    const = lambda shape: pl.BlockSpec(shape, lambda b, t: (0, 0))
    in_specs = [pl.BlockSpec((tq, W), lambda b, t: (b * nt + t, cq)),
                pl.BlockSpec((L, 128), lambda b, t: (b, ck)),
                pl.BlockSpec((L, 128), lambda b, t: (b, cv)),
                const((1, W)), const((1, 128))]
    args = [p, p, p, qn, kn]
    if rope:
        cos, sin = rope_tabs
        in_specs += [pl.BlockSpec((tq, W), lambda b, t: (t, 0)), pl.BlockSpec((tq, W), lambda b, t: (t, 0)),
                     const((L, 128)), const((L, 128))]
        args += [cos, sin, cos, sin]
    if cached:
        kc, vc = cache
        P = kc.shape[1]
        in_specs += [pl.BlockSpec((1, P, 128), lambda b, t: (b, 0, 0))] * 2
        args += [kc, vc]
    return pl.pallas_call(
        functools.partial(_attn_kernel, rope=rope, cached=cached, tq=tq, kchunk=kchunk),
        grid=(B, nt),
        in_specs=in_specs,
        out_specs=[pl.BlockSpec((tq, W), lambda b, t: (b * nt + t, 0)),
                   pl.BlockSpec((L, 128), lambda b, t: (b, 0))],
        out_shape=[jax.ShapeDtypeStruct((B * L, W), F32), jax.ShapeDtypeStruct((B * L, 128), F32)],
        scratch_shapes=[pltpu.VMEM((L, 128), BF16)],
        compiler_params=_cparams(("arbitrary", "arbitrary")),
        name="attn",
    )(*args)


def _rope_tables(L):
    t = jnp.arange(L)
    pos_r = (t // GRID_W).astype(F32)[:, None]
    pos_c = (t % GRID_W).astype(F32)[:, None]
    nf = HD // 4
    inv = ROPE_THETA ** (-jnp.arange(nf, dtype=F32) / nf)[None, :]
    ar, ac = pos_r * inv, pos_c * inv
    cos = jnp.concatenate([jnp.cos(ar), jnp.cos(ar), jnp.cos(ac), jnp.cos(ac)], axis=1)
    sin = jnp.concatenate([-jnp.sin(ar), jnp.sin(ar), -jnp.sin(ac), jnp.sin(ac)], axis=1)
    return jnp.tile(cos, (1, NH)), jnp.tile(sin, (1, NH))


def _layer(x, B, L, mod8, base, rows_per_cond, lp, consts, ctx):
    p = _inproj(x, mod8, lp['w_in_p'], rows_per_cond, base)
    y_hy = _hyena(p, B, L, consts['hy'], lp)

    if ctx is None:
        c0 = jnp.zeros((B, 8, HD, HD), F32)
        n0 = jnp.zeros((B, 8, HD), F32)
        m0 = jnp.zeros((B, 8, 128), F32)
        s0 = jnp.zeros((B, 8, HD, HD), F32)
        cache = None
    else:
        kc, vc, c0, n0, m0, s0 = ctx
        P = kc.shape[1]
        cache = (kc.reshape(B, P, 128), vc.reshape(B, P, 128))
        c0 = c0.reshape(B, 8, HD, HD)
        n0 = n0.reshape(B, 8, HD)
        m0 = jnp.broadcast_to(m0.reshape(B, 8, 1), (B, 8, 128))
        s0 = s0.reshape(B, 8, HD, HD)

    gbias = jnp.pad(lp['ml_gate_b'], (0, 128 - 4 * NH)).reshape(1, 128)
    hf, hb, C, n, m = _mlstm_scan(p, B, L, gbias, c0, n0, m0)
    y_ml = _mlfin(hf, hb, p, lp['ml_norm_g'].reshape(1, W))

    r, v, kk, g, bon, lw0, lw1, kt0, kt1, ba0, ba1 = _rwprep(p, B, L, lp)
    yf, yb, S = _rwscan((r, v, kk, lw0, lw1, kt0, kt1, ba0, ba1), B, L, s0)
    y_rw = _rwfin(yf, yb, bon, g, lp['rw_ln_g'].reshape(1, W), lp['rw_ln_b'].reshape(1, W))

    qn = jnp.tile(lp['at_qn'], NH).reshape(1, W)
    kn = jnp.tile(lp['at_kn'], 2).reshape(1, 128)
    y_at, k_new = _attn(p, B, L, qn, kn, consts.get('rope'), cache)
    v_new = p[:, C_AT + W + 128:C_AT + W + 256]

    row = lambda a: a.reshape(1, D)
    x = _outmlp((y_hy, y_ml, y_rw, y_at), x, mod8, lp['w_out_b'], row(lp['ln1_g']), row(lp['ln1_b']),
                lp['mlp_w1_b'], lp['mlp_w2_b'], row(lp['ln2_g']), row(lp['ln2_b']), rows_per_cond, base)
    state = (k_new.reshape(B, L, 2, HD), v_new.reshape(B, L, 2, HD), C.reshape(B, 2, NH, HD, HD),
             n.reshape(B, 2, NH, HD), m[:, :, 0].reshape(B, 2, NH), S.reshape(B, 2, NH, HD, HD))
    return x, state


def _permute_w_in(w_in):
    hy = w_in[:, :, 0:768]
    ml = w_in[:, :, 768:1792]
    mlg = w_in[:, :, 1792:1808]
    rw = w_in[:, :, 1808:2832]
    at = w_in[:, :, 2832:3344]
    mlg = jnp.pad(mlg, ((0, 0), (0, 0), (0, 128 - 16)))
    return jnp.concatenate([hy, ml, rw, at, mlg], axis=-1).astype(BF16)


def kernel(x_prompt, x_sample, cache_attn_k, cache_attn_v, state_mlstm_C, state_mlstm_n, state_mlstm_m, state_rwkv_S, c, c_ctx, w_mod, b_mod, w_in, hy_conv, hy_w1, hy_b1, hy_freq, hy_w2, hy_b2, hy_w3, hy_bias, ml_gate_b, ml_norm_g, rw_mu, rw_w0, rw_w2, rw_a0, rw_a2, rw_g2, rw_kk, rw_ka, rw_rk, rw_ln_g, rw_ln_b, at_qn, at_kn, w_out, ln1_g, ln1_b, mlp_w1, mlp_w2, ln2_g, ln2_b):
    Bp, Lp, _ = x_prompt.shape
    Bs, Ls, _ = x_sample.shape
    params = {
        'w_in_p': _permute_w_in(w_in), 'hy_conv': hy_conv, 'hy_w1': hy_w1, 'hy_b1': hy_b1, 'hy_freq': hy_freq,
        'hy_w2': hy_w2, 'hy_b2': hy_b2, 'hy_w3': hy_w3, 'hy_bias': hy_bias,
        'ml_gate_b': ml_gate_b, 'ml_norm_g': ml_norm_g,
        'rw_mu': rw_mu, 'rw_w0': rw_w0, 'rw_w2': rw_w2, 'rw_a0': rw_a0, 'rw_a2': rw_a2, 'rw_g2': rw_g2,
        'rw_kk': rw_kk, 'rw_ka': rw_ka, 'rw_rk': rw_rk, 'rw_ln_g': rw_ln_g, 'rw_ln_b': rw_ln_b,
        'at_qn': at_qn, 'at_kn': at_kn, 'w_out_b': w_out.astype(BF16), 'ln1_g': ln1_g, 'ln1_b': ln1_b,
        'mlp_w1_b': mlp_w1.astype(BF16), 'mlp_w2_b': mlp_w2.astype(BF16), 'ln2_g': ln2_g, 'ln2_b': ln2_b,
    }
    cvec = jnp.concatenate([c_ctx[None, :], c, jnp.zeros((8 - 1 - Bs, D), F32)], axis=0)
    mod = _mod_all(cvec, w_mod, b_mod).reshape(DEPTH, 8, 1, 6 * D)

    consts_p = {'hy': _hyena_consts(Lp) + _dft_consts(Lp)}
    consts_s = {'hy': _hyena_consts(Ls) + _dft_consts(Ls), 'rope': _rope_tables(Ls)}

    x = x_prompt.reshape(Bp * Lp, D)
    states = []
    for l in range(DEPTH):
        lp = {name: arr[l] for name, arr in params.items()}
        x, st = _layer(x, Bp, Lp, mod[l], 0, Bp * Lp, lp, consts_p, None)
        states.append(st)
    y_prompt = x.reshape(Bp, Lp, D)
    outs = [jnp.stack([st[i] for st in states], axis=1) for i in range(6)]

    x = x_sample.reshape(Bs * Ls, D)
    for l in range(DEPTH):
        lp = {name: arr[l] for name, arr in params.items()}
        ctx = (cache_attn_k[:, l], cache_attn_v[:, l], state_mlstm_C[:, l], state_mlstm_n[:, l],
               state_mlstm_m[:, l], state_rwkv_S[:, l])
        x, _ = _layer(x, Bs, Ls, mod[l], 1, Ls, lp, consts_s, ctx)
    y_sample = x.reshape(Bs, Ls, D)
    return (y_prompt, y_sample) + tuple(outs)
```

```python
import functools
import math

import jax
import jax.numpy as jnp
from jax import lax
from jax.experimental import pallas as pl
from jax.experimental.pallas import tpu as pltpu

F32 = jnp.float32
BF16 = jnp.bfloat16
HI = lax.Precision.HIGHEST

D = 1024
DEPTH = 4
W = 256
HD = 64
NH = 4
GRID_W = 64
D_FF = 4 * D
ALPHA = (2.0 * DEPTH) ** 0.25
HY_BANDS = 16
HY_EMB = 2 * HY_BANDS + 1
HY_FFN = 64
HY_MAX_DECAY = math.log(1e-2) / 0.3
HY_MIN_DECAY = math.log(1e-2) / 1.5
ML_CHUNK = 128
RW_CHUNK = 64
RW_DECAY_SCALE = 0.606531
RW_GN_EPS = 64e-5
ROPE_THETA = 10000.0

PC = 3456
C_HY = 0
C_ML = 768
C_RW = 1792
C_RWX = 2560
C_AT = 2816
C_MLG = 3328

VMEM_LIMIT = 56 * 1024 * 1024


def _cparams(sem):
    return pltpu.CompilerParams(dimension_semantics=sem, vmem_limit_bytes=VMEM_LIMIT)


_NN = (((1,), (0,)), ((), ()))
_NT = (((1,), (1,)), ((), ()))


def _dot(a, b, dn, hi):
    if hi:
        return lax.dot_general(a, b, dn, preferred_element_type=F32, precision=HI)
    return lax.dot_general(a.astype(BF16), b.astype(BF16), dn, preferred_element_type=F32)


def _mm(a, b, hi=True):
    return _dot(a, b, _NN, hi)


def _mm_nt(a, b, hi=True):
    return _dot(a, b, _NT, hi)


def _pieces(x, n):
    out = []
    for _ in range(n - 1):
        p = x.astype(BF16)
        out.append(p)
        x = x - p.astype(F32)
    out.append(x.astype(BF16))
    return out


def _sel_mm(sel, x, n=3):
    acc = None
    for p in _pieces(x, n):
        t = jnp.dot(sel, p, preferred_element_type=F32)
        acc = t if acc is None else acc + t
    return acc


def _mm_sel(x, sel, n=2):
    acc = None
    for p in _pieces(x, n):
        t = jnp.dot(p, sel, preferred_element_type=F32)
        acc = t if acc is None else acc + t
    return acc


def _head_ones(width):
    r = lax.broadcasted_iota(jnp.int32, (width, width), 0) >> 6
    c = lax.broadcasted_iota(jnp.int32, (width, width), 1) >> 6
    return (r == c).astype(BF16)


def _layernorm(x, g, b):
    mu = jnp.mean(x, -1, keepdims=True)
    xc = x - mu
    var = jnp.mean(xc * xc, -1, keepdims=True)
    return xc * lax.rsqrt(var + 1e-5) * g + b


def _head_norm(x, eps):
    ones = _head_ones(x.shape[1])
    mu = _mm_sel(x, ones) * (1.0 / HD)
    xc = x - mu
    var = _mm_sel(xc * xc, ones) * (1.0 / HD)
    return xc * lax.rsqrt(var + eps)


def _mod_kernel(c_ref, w_ref, b_ref, o_ref):
    c = c_ref[...]
    s = c * jax.nn.sigmoid(c)
    o_ref[0] = _mm(s, w_ref[0]) + b_ref[0]


def _mod_all(cvec, w_mod, b_mod):
    tn = 1536
    return pl.pallas_call(
        _mod_kernel,
        grid=(DEPTH, 6 * D // tn),
        in_specs=[pl.BlockSpec((8, D), lambda l, n: (0, 0)),
                  pl.BlockSpec((1, D, tn), lambda l, n: (l, 0, n)),
                  pl.BlockSpec((1, 1, tn), lambda l, n: (l, 0, n))],
        out_specs=pl.BlockSpec((1, 8, tn), lambda l, n: (l, 0, n)),
        out_shape=jax.ShapeDtypeStruct((DEPTH, 8, 6 * D), F32),
        compiler_params=_cparams(("arbitrary", "arbitrary")),
        name="mod",
    )(cvec, w_mod, b_mod.reshape(DEPTH, 1, 6 * D))


def _mod_row_map(rows_per_cond, tm, base):
    return lambda i: (base + (i * tm) // rows_per_cond, 0, 0)


def _inproj_kernel(x_ref, mod_ref, w_ref, o_ref):
    m = mod_ref[0]
    h = x_ref[...] * (1.0 + m[:, D:2 * D]) + m[:, 0:D]
    o_ref[...] = jnp.dot(h.astype(BF16), w_ref[...], preferred_element_type=F32)


def _inproj(x, mod8, w_in_p, rows_per_cond, base):
    n = x.shape[0]
    tm, tn = 512, 1152
    rmap = _mod_row_map(rows_per_cond, tm, base)
    return pl.pallas_call(
        _inproj_kernel,
        grid=(PC // tn, n // tm),
        in_specs=[pl.BlockSpec((tm, D), lambda c, i: (i, 0)),
                  pl.BlockSpec((1, 1, 6 * D), lambda c, i: rmap(i)),
                  pl.BlockSpec((D, tn), lambda c, i: (0, c))],
        out_specs=pl.BlockSpec((tm, tn), lambda c, i: (i, c)),
        out_shape=jax.ShapeDtypeStruct((n, PC), F32),
        compiler_params=_cparams(("arbitrary", "arbitrary")),
        name="inproj",
    )(x, mod8, w_in_p)


def _outmlp_kernel(yh, ym, yr, ya, x_ref, mod_ref, wo, g1, b1, w1, w2, g2, b2, o_ref):
    m = mod_ref[0]
    y = jnp.concatenate([yh[...], ym[...], yr[...], ya[...]], axis=-1).astype(BF16)
    mix = jnp.dot(y, wo[...], preferred_element_type=F32)
    x1 = _layernorm(ALPHA * x_ref[...] + m[:, 2 * D:3 * D] * mix, g1[...], b1[...])
    h = (x1 * (1.0 + m[:, 4 * D:5 * D]) + m[:, 3 * D:4 * D]).astype(BF16)
    acc = jnp.zeros(x1.shape, F32)
    for c in range(D_FF // D):
        a = jnp.dot(h, w1[:, c * D:(c + 1) * D], preferred_element_type=F32)
        a = jnp.square(jnp.maximum(a, 0.0)).astype(BF16)
        acc = acc + jnp.dot(a, w2[c * D:(c + 1) * D, :], preferred_element_type=F32)
    o_ref[...] = _layernorm(ALPHA * x1 + m[:, 5 * D:6 * D] * acc, g2[...], b2[...])


def _outmlp(ys, x, mod8, wo, g1, b1, w1, w2, g2, b2, rows_per_cond, base):
    n = x.shape[0]
    tm = 512
    rmap = _mod_row_map(rows_per_cond, tm, base)
    const = lambda shape: pl.BlockSpec(shape, lambda i: (0, 0), pipeline_mode=pl.Buffered(1))
    row = lambda wd: pl.BlockSpec((tm, wd), lambda i: (i, 0))
    return pl.pallas_call(
        _outmlp_kernel,
        grid=(n // tm,),
        in_specs=[row(W), row(W), row(W), row(W), row(D),
                  pl.BlockSpec((1, 1, 6 * D), lambda i: rmap(i)),
                  const((D, D)), const((1, D)), const((1, D)),
                  const((D, D_FF)), const((D_FF, D)), const((1, D)), const((1, D))],
        out_specs=row(D),
        out_shape=jax.ShapeDtypeStruct((n, D), F32),
        compiler_params=_cparams(("arbitrary",)),
        name="outmlp",
    )(*ys, x, mod8, wo, g1, b1, w1, w2, g2, b2)


def _hyfilt_kernel(z_ref, w1, b1, fr, w2, b2, w3, dec_ref, h_ref, nrm_ref):
    i = pl.program_id(0)
    f = fr[...]
    hid = jnp.sin(f * (_mm(z_ref[...], w1[...]) + b1[...]))
    hid = jnp.sin(f * (_mm(hid, w2[...]) + b2[...]))
    h = _mm(hid, w3[...])
    dec = dec_ref[...]
    h = h * jnp.concatenate([dec, dec, dec, dec], axis=1)
    rid = lax.broadcasted_iota(jnp.int32, h.shape, 0)
    cid = lax.broadcasted_iota(jnp.int32, h.shape, 1)
    anti = ((cid >> 8) & 1) == 1
    h = jnp.where(anti & (rid == 0) & (i == 0), 0.0, h)
    h_ref[...] = h.astype(BF16)
    part = jnp.broadcast_to(jnp.sum(jnp.abs(h), axis=0, keepdims=True), (8, 4 * W))

    @pl.when(i == 0)
    def _():
        nrm_ref[...] = part

    @pl.when(i > 0)
    def _():
        nrm_ref[...] = nrm_ref[...] + part


def _hyfilt(L, z, w1p, b1, fr, w2, b2, w3, dec):
    tl = min(L, 512)
    const = lambda shape: pl.BlockSpec(shape, lambda i: (0, 0))
    return pl.pallas_call(
        _hyfilt_kernel,
        grid=(L // tl,),
        in_specs=[pl.BlockSpec((tl, 128), lambda i: (i, 0)),
                  const((128, HY_FFN)), const((1, HY_FFN)), const((1, HY_FFN)),
                  const((HY_FFN, HY_FFN)), const((1, HY_FFN)), const((HY_FFN, 4 * W)),
                  pl.BlockSpec((tl, W), lambda i: (i, 0))],
        out_specs=[pl.BlockSpec((tl, 4 * W), lambda i: (i, 0)), const((8, 4 * W))],
        out_shape=[jax.ShapeDtypeStruct((L, 4 * W), BF16), jax.ShapeDtypeStruct((8, 4 * W), F32)],
        compiler_params=_cparams(("arbitrary",)),
        name="hyfilt",
    )(z, w1p, b1, fr, w2, b2, w3, dec)


def _hyspec_kernel(fc, fs, h_ref, nrm_ref, re_ref, im_ref, *, L):
    k = pl.program_id(0)
    h = h_ref[...]
    re = jnp.dot(fc[...], h, preferred_element_type=F32)
    im = jnp.dot(fs[...], h, preferred_element_type=F32)
    tk = re.shape[0]
    rid = lax.broadcasted_iota(jnp.int32, (tk, W), 0) + k * tk
    packed = rid == 0
    scale = jnp.where(packed, 1.0, 2.0) * (1.0 / (2 * L))
    nrm = nrm_ref[0:1, :]
    res, ims = [], []
    for f in range(2):
        c0, c1 = 2 * f * W, (2 * f + 1) * W
        inv = scale / (nrm[:, c0:c0 + W] + nrm[:, c1:c1 + W])
        res.append((re[:, c0:c0 + W] + re[:, c1:c1 + W]) * inv)
        ims.append(jnp.where(packed, im[:, c0:c0 + W] + im[:, c1:c1 + W],
                             im[:, c0:c0 + W] - im[:, c1:c1 + W]) * inv)
    re_ref[...] = jnp.concatenate(res, axis=1)
    im_ref[...] = jnp.concatenate(ims, axis=1)


def _hyspec(L, fc, fs, h, nrm):
    tk = min(L, 512)
    return pl.pallas_call(
        functools.partial(_hyspec_kernel, L=L),
        grid=(L // tk,),
        in_specs=[pl.BlockSpec((tk, L), lambda k: (k, 0)),
                  pl.BlockSpec((tk, L), lambda k: (k, 0)),
                  pl.BlockSpec((L, 4 * W), lambda k: (0, 0), pipeline_mode=pl.Buffered(1)),
                  pl.BlockSpec((8, 4 * W), lambda k: (0, 0))],
        out_specs=[pl.BlockSpec((tk, 2 * W), lambda k: (k, 0))] * 2,
        out_shape=[jax.ShapeDtypeStruct((L, 2 * W), F32)] * 2,
        compiler_params=_cparams(("arbitrary",)),
        name="hyspec",
    )(fc, fs, h, nrm)


def _hypre_kernel(x_ref, w_ref, o_ref):
    x = x_ref[...]
    L = x.shape[0]
    w = w_ref[0]
    rid = lax.broadcasted_iota(jnp.int32, x.shape, 0)
    xp = jnp.where(rid == 0, 0.0, pltpu.roll(x, 1, axis=0))
    xn = jnp.where(rid == L - 1, 0.0, pltpu.roll(x, L - 1, axis=0))
    o_ref[...] = xp * w[0:1, :] + x * w[1:2, :] + xn * w[2:3, :]


def _hypre(p, B, L, conv_w3):
    return pl.pallas_call(
        _hypre_kernel,
        grid=(B, 3),
        in_specs=[pl.BlockSpec((L, W), lambda b, c: (b, c)),
                  pl.BlockSpec((1, 3, W), lambda b, c: (c, 0, 0))],
        out_specs=pl.BlockSpec((L, W), lambda b, c: (b, c)),
        out_shape=jax.ShapeDtypeStruct((B * L, 3 * W), F32),
        compiler_params=_cparams(("arbitrary", "arbitrary")),
        name="hypre",
    )(p, conv_w3)


def _hyfwd_kernel(u_ref, fc, fs, fre_ref, fim_ref, yre_ref, yim_ref, ub):
    k = pl.program_id(1)

    @pl.when(k == 0)
    def _():
        ub[...] = u_ref[...].astype(BF16)

    u = ub[...]
    re = jnp.dot(fc[...], u, preferred_element_type=F32)
    im = jnp.dot(fs[...], u, preferred_element_type=F32)
    fre, fim = fre_ref[...], fim_ref[...]
    tk = re.shape[0]
    packed = (lax.broadcasted_iota(jnp.int32, re.shape, 0) + k * tk) == 0
    yre_ref[...] = jnp.where(packed, re * fre, re * fre - im * fim)
    yim_ref[...] = jnp.where(packed, im * fim, re * fim + im * fre)


def _hyfwd(u, ucol, B, L, fc, fs, fre, fim, f):
    tk = min(L, 512)
    return pl.pallas_call(
        _hyfwd_kernel,
        grid=(B, L // tk),
        in_specs=[pl.BlockSpec((L, W), lambda b, k: (b, ucol)),
                  pl.BlockSpec((tk, L), lambda b, k: (k, 0)),
                  pl.BlockSpec((tk, L), lambda b, k: (k, 0)),
                  pl.BlockSpec((tk, W), lambda b, k: (k, f)),
                  pl.BlockSpec((tk, W), lambda b, k: (k, f))],
        out_specs=[pl.BlockSpec((tk, W), lambda b, k: (b * (L // tk) + k, 0))] * 2,
        out_shape=[jax.ShapeDtypeStruct((B * L, W), F32)] * 2,
        scratch_shapes=[pltpu.VMEM((L, W), BF16)],
        compiler_params=_cparams(("arbitrary", "arbitrary")),
        name="hyfwd",
    )(u, fc, fs, fre, fim)


def _hyinv_kernel(yre_ref, yim_ref, fc, fst, u_ref, g_ref, bias_ref, o_ref, yb):
    t = pl.program_id(1)

    @pl.when(t == 0)
    def _():
        yb[0] = yre_ref[...].astype(BF16)
        yb[1] = yim_ref[...].astype(BF16)

    y = jnp.dot(fc[...], yb[0], preferred_element_type=F32) + jnp.dot(fst[...], yb[1], preferred_element_type=F32)
    o_ref[...] = g_ref[...] * (y + u_ref[...] * bias_ref[0])


def _hyinv(yre, yim, B, L, fc, fst, u, ucol, gate, gcol, bias, f):
    tt = min(L, 512)
    nt = L // tt
    return pl.pallas_call(
        _hyinv_kernel,
        grid=(B, nt),
        in_specs=[pl.BlockSpec((L, W), lambda b, t: (b, 0)),
                  pl.BlockSpec((L, W), lambda b, t: (b, 0)),
                  pl.BlockSpec((tt, L), lambda b, t: (t, 0)),
                  pl.BlockSpec((tt, L), lambda b, t: (t, 0)),
                  pl.BlockSpec((tt, W), lambda b, t: (b * nt + t, ucol)),
                  pl.BlockSpec((tt, W), lambda b, t: (b * nt + t, gcol)),
                  pl.BlockSpec((1, 1, W), lambda b, t: (f, 0, 0))],
        out_specs=pl.BlockSpec((tt, W), lambda b, t: (b * nt + t, 0)),
        out_shape=jax.ShapeDtypeStruct((B * L, W), F32),
        scratch_shapes=[pltpu.VMEM((2, L, W), BF16)],
        compiler_params=_cparams(("arbitrary", "arbitrary")),
        name="hyinv",
    )(yre, yim, fc, fst, u, gate, bias)


def _dft_consts(L):
    s = int(round(math.sqrt(L)))
    assert s * s == L
    k = jnp.arange(L, dtype=jnp.int32)[:, None]
    n1 = jnp.arange(s, dtype=jnp.int32)[None, :]
    a1 = ((k * (n1 * s)) % (2 * L)).astype(F32) * (math.pi / L)
    a2 = ((k * n1) % (2 * L)).astype(F32) * (math.pi / L)
    c1, s1, c2, s2 = jnp.cos(a1), jnp.sin(a1), jnp.cos(a2), jnp.sin(a2)
    fc = (c1[:, :, None] * c2[:, None, :] - s1[:, :, None] * s2[:, None, :]).reshape(L, L)
    fs = -(s1[:, :, None] * c2[:, None, :] + c1[:, :, None] * s2[:, None, :]).reshape(L, L)
    nyq = (1.0 - 2.0 * (jnp.arange(L) % 2)).astype(F32)[None, :]
    fs = jnp.where(k == 0, nyq, fs)
    return fc.astype(BF16), fs.astype(BF16), fs.T.astype(BF16)


def _hyena_consts(L):
    t = jnp.linspace(0.0, 1.0, L, dtype=F32)[:, None]
    ang = (2.0 * math.pi / L) * jnp.arange(L, dtype=F32)[:, None]
    bands = jnp.linspace(1e-4, HY_BANDS - 1, HY_BANDS, dtype=F32)[None, :]
    z = jnp.concatenate([t, jnp.cos(bands * ang), jnp.sin(bands * ang)], axis=-1)
    z = jnp.pad(z, ((0, 0), (0, 128 - HY_EMB)))
    deltas = jnp.abs(jnp.linspace(HY_MIN_DECAY, HY_MAX_DECAY, W, dtype=F32))
    dec = jnp.exp(-t * deltas)
    return z, dec


def _hyena(p, B, L, hc, lp):
    z, dec, fc, fs, fst = hc
    w1p = jnp.pad(lp['hy_w1'], ((0, 128 - HY_EMB), (0, 0)))
    row = lambda a: a.reshape(1, -1)
    h, nrm = _hyfilt(L, z, w1p, row(lp['hy_b1']), row(lp['hy_freq']), lp['hy_w2'], row(lp['hy_b2']),
                     lp['hy_w3'], dec)
    fre, fim = _hyspec(L, fc, fs, h, nrm)
    pc = _hypre(p, B, L, lp['hy_conv'].reshape(3, 3, W).transpose(1, 0, 2))
    bias = lp['hy_bias'].reshape(2, 1, W)
    yre, yim = _hyfwd(pc, 0, B, L, fc, fs, fre, fim, 0)
    zmid = _hyinv(yre, yim, B, L, fc, fst, pc, 0, pc, 1, bias, 0)
    yre, yim = _hyfwd(zmid, 0, B, L, fc, fs, fre, fim, 1)
    return _hyinv(yre, yim, B, L, fc, fst, zmid, 0, pc, 2, bias, 1)


def _mlstm_kernel(qf, kf, vf, gf, qb, kb, vb, gb, gbias, c0, n0, m0,
                  hf_o, hb_o, c_o, n_o, m_o, c_s, n_s, m_s, *, nc, nb):
    j = pl.program_id(1)
    T = ML_CHUNK

    @pl.when(j == 0)
    def _():
        c_s[...] = c0[...]
        n_s[...] = n0[...]
        m_s[...] = m0[...]

    lane = lax.broadcasted_iota(jnp.int32, (T, 128), 1)
    is_f = ((lane & 4) != 0) & (lane < 16)
    row = lax.broadcasted_iota(jnp.int32, (T, T), 0)
    col = lax.broadcasted_iota(jnp.int32, (T, T), 1)
    masks = (row >= col, row <= col)
    refs = ((qf, kf, vf, gf), (qb, kb, vb, gb))
    grp = [(d, i) for d in range(2) for i in range(nb)]
    Gs = []
    for d, i in grp:
        g = refs[d][3][i] + gbias[...]
        Gs.append(jnp.where(is_f, jnp.minimum(g, 0.0) - jnp.log1p(jnp.exp(-jnp.abs(g))), g))
    Bcs = [_sel_mm(masks[d].astype(BF16), G) for (d, i), G in zip(grp, Gs)]
    units = []
    for gi, (d, i) in enumerate(grp):
        G, Bc = Gs[gi], Bcs[gi]
        GT, BcT = G.T, Bc.T
        q32 = refs[d][0][i] * (HD ** -0.5)
        q = q32.astype(BF16)
        k = refs[d][1][i]
        v = refs[d][2][i].astype(BF16)
        for h in range(NH):
            cf, ci = d * 8 + 4 + h, d * 8 + h
            sl = slice(h * HD, (h + 1) * HD)
            units.append(dict(
                d=d, i=i, u=d * NH + h, q=q[:, sl], q32=q32[:, sl], k=k[:, sl], v=v[:, sl],
                b_col=Bc[:, cf:cf + 1], b_row=BcT[cf:cf + 1, :], i_col=G[:, ci:ci + 1], i_row=GT[ci:ci + 1, :],
                b_tot=Bc[T - 1:T, cf:cf + 1] if d == 0 else Bc[0:1, cf:cf + 1]))
    for un in units:
        un['m_prev'] = m_s[un['i'], un['u']:un['u'] + 1, 0:1]
        un['C'] = c_s[un['i'], un['u']]
        un['n'] = n_s[un['i'], un['u']:un['u'] + 1, :]
    for un in units:
        un['qk'] = _mm_nt(un['q'], un['k'], False)
    for un in units:
        un['qC'] = _mm(un['q'], un['C'], False)
    for un in units:
        logd = jnp.where(masks[un['d']], un['b_col'] - un['b_row'] + un['i_row'], -jnp.inf)
        inter = un['b_col'] + un['m_prev']
        m_t = jnp.maximum(inter, jnp.max(logd, axis=-1, keepdims=True))
        un['s'] = un['qk'] * jnp.exp(logd - m_t)
        un['w_in'] = jnp.exp(inter - m_t)
        un['floor'] = jnp.exp(-m_t)
        logw_col = un['b_tot'] - un['b_col'] + un['i_col']
        m_new = jnp.maximum(un['b_tot'] + un['m_prev'], jnp.max(logw_col, axis=0, keepdims=True))
        un['kw'] = un['k'] * jnp.exp(logw_col - m_new)
        un['decay'] = jnp.exp(un['b_tot'] + un['m_prev'] - m_new)
        un['m_new'] = m_new
    for un in units:
        un['sv'] = _mm(un['s'], un['v'], False)
    for un in units:
        un['kv'] = _mm(un['kw'].T, un['v'], False)
    hs = {}
    for un in units:
        i, u = un['i'], un['u']
        num = un['sv'] + un['w_in'] * un['qC']
        den = jnp.sum(un['s'], axis=-1, keepdims=True) \
            + un['w_in'] * jnp.sum(un['q32'] * un['n'], axis=-1, keepdims=True)
        hs.setdefault((un['d'], i), []).append(num / jnp.maximum(jnp.abs(den), un['floor']))
        c_s[i, u] = un['decay'] * un['C'] + un['kv']
        n_s[i, u:u + 1, :] = un['decay'] * un['n'] + jnp.sum(un['kw'], axis=0, keepdims=True)
        m_s[i, u:u + 1, :] = jnp.broadcast_to(un['m_new'], (1, 128))
    for (d, i), parts in hs.items():
        (hf_o, hb_o)[d][i] = jnp.concatenate(parts, axis=1)

    @pl.when(j == nc - 1)
    def _():
        c_o[...] = c_s[...]
        n_o[...] = n_s[...]
        m_o[...] = m_s[...]


def _mlstm_scan(p, B, L, gbias, c0, n0, m0):
    T = ML_CHUNK
    nc = L // T
    nb = 2
    p3 = p.reshape(B, L, PC)
    fwd = lambda cb: (lambda b, j: (b, j, cb))
    bwd = lambda cb: (lambda b, j: (b, nc - 1 - j, cb))
    blk = lambda f, cb, wd=W: pl.BlockSpec((nb, T, wd), f(cb))
    cq, ck, cv, cg = C_ML // W, C_ML // W + 1, C_ML // W + 2, C_MLG // 128
    st = lambda shape: pl.BlockSpec((nb,) + shape, lambda b, j: (b,) + (0,) * len(shape))
    hf, hb, C, n, m = pl.pallas_call(
        functools.partial(_mlstm_kernel, nc=nc, nb=nb),
        grid=(B // nb, nc),
        in_specs=[blk(fwd, cq), blk(fwd, ck), blk(fwd, cv), blk(fwd, cg, 128),
                  blk(bwd, cq), blk(bwd, ck), blk(bwd, cv), blk(bwd, cg, 128),
                  pl.BlockSpec((1, 128), lambda b, j: (0, 0)),
                  st((8, HD, HD)), st((8, HD)), st((8, 128))],
        out_specs=[pl.BlockSpec((nb, T, W), lambda b, j: (b, j, 0)),
                   pl.BlockSpec((nb, T, W), lambda b, j: (b, nc - 1 - j, 0)),
                   st((8, HD, HD)), st((8, HD)), st((8, 128))],
        out_shape=[jax.ShapeDtypeStruct((B, L, W), F32), jax.ShapeDtypeStruct((B, L, W), F32),
                   jax.ShapeDtypeStruct((B, 8, HD, HD), F32), jax.ShapeDtypeStruct((B, 8, HD), F32),
                   jax.ShapeDtypeStruct((B, 8, 128), F32)],
        scratch_shapes=[pltpu.VMEM((nb, 8, HD, HD), F32), pltpu.VMEM((nb, 8, HD), F32),
                        pltpu.VMEM((nb, 8, 128), F32)],
        compiler_params=_cparams(("arbitrary", "arbitrary")),
        name="mlstm",
    )(p3, p3, p3, p3, p3, p3, p3, p3, gbias, c0, n0, m0)
    return hf.reshape(B * L, W), hb.reshape(B * L, W), C, n, m


def _mlfin_kernel(hf, hb, o_ref, g_ref, y_ref):
    h = _head_norm(hf[...] + hb[...], 1e-6) * g_ref[...]
    y_ref[...] = h * jax.nn.sigmoid(o_ref[...])


def _mlfin(hf, hb, p, norm_g):
    n = hf.shape[0]
    tm = 512
    row = lambda cb: pl.BlockSpec((tm, W), lambda i: (i, cb))
    return pl.pallas_call(
        _mlfin_kernel,
        grid=(n // tm,),
        in_specs=[row(0), row(0), row(C_ML // W + 3), pl.BlockSpec((1, W), lambda i: (0, 0))],
        out_specs=row(0),
        out_shape=jax.ShapeDtypeStruct((n, W), F32),
        compiler_params=_cparams(("arbitrary",)),
        name="mlfin",
    )(hf, hb, p, norm_g)


def _shifted(x, prev8, next8, first, last):
    T = x.shape[0]
    rid = lax.broadcasted_iota(jnp.int32, x.shape, 0)
    pr = jnp.where(first, 0.0, prev8[7:8, :])
    nx = jnp.where(last, 0.0, next8[0:1, :])
    xp = jnp.where(rid == 0, pr, pltpu.roll(x, 1, axis=0))
    xn = jnp.where(rid == T - 1, nx, pltpu.roll(x, T - 1, axis=0))
    return xp, xn


def _rwprep_kernel(*refs, L, T):
    xs, prevs, nexts = refs[0:4], refs[4:8], refs[8:12]
    mu, w0, w2, a0, a2, g2, kkw, kaw, rkw = refs[12:21]
    r_o, v_o, kk_o, g_o, bon_o, lw0_o, lw1_o, kt0_o, kt1_o, ba0_o, ba1_o = refs[21:]
    i = pl.program_id(0)
    first = (i * T) % L == 0
    last = ((i + 1) * T) % L == 0
    pf = []
    for c in range(4):
        x = xs[c][...]
        xp, xn = _shifted(x, prevs[c][...], nexts[c][...], first, last)
        pf.append(x + mu[:, c * W:(c + 1) * W] * (0.5 * (xp + xn) - x))
    r, k, v, misc = pf
    lw, la, lg = misc[:, 0:64], misc[:, 64:128], misc[:, 128:256]
    ones = _head_ones(W)
    g_o[...] = _mm(jax.nn.sigmoid(lg), g2[...])
    kq = k * kkw[...]
    kk = kq / jnp.maximum(jnp.sqrt(_mm_sel(kq * kq, ones)), 1e-12)
    tw = jnp.tanh(lw)
    r_o[...] = r
    v_o[...] = v
    kk_o[...] = kk
    bon = jnp.zeros_like(r)
    for d, (lw_o, kt_o, ba_o) in enumerate(((lw0_o, kt0_o, ba0_o), (lw1_o, kt1_o, ba1_o))):
        lw_o[...] = -RW_DECAY_SCALE * jax.nn.sigmoid(w0[d:d + 1, :] + _mm(tw, w2[d]))
        a = jax.nn.sigmoid(a0[d:d + 1, :] + _mm(la, a2[d]))
        kt = k * (1.0 + (a - 1.0) * kaw[...])
        kt_o[...] = kt
        ba_o[...] = kk * a
        bon = bon + _mm_sel(r * kt * rkw[...], ones) * v
    bon_o[...] = bon


def _rwprep(p, B, L, lp):
    n = B * L
    T = min(L, 512)
    nb8 = n // 8
    c0 = C_RW // W
    cur = lambda c: pl.BlockSpec((T, W), lambda i: (i, c0 + c))
    prv = lambda c: pl.BlockSpec((8, W), lambda i: (jnp.maximum(i * (T // 8) - 1, 0), c0 + c))
    nxt = lambda c: pl.BlockSpec((8, W), lambda i: (jnp.minimum((i + 1) * (T // 8), nb8 - 1), c0 + c))
    const = lambda shape: pl.BlockSpec(shape, lambda i: (0,) * len(shape))
    row = lambda a: a.reshape(1, -1)
    out = pl.BlockSpec((T, W), lambda i: (i, 0))
    return pl.pallas_call(
        functools.partial(_rwprep_kernel, L=L, T=T),
        grid=(n // T,),
        in_specs=[cur(c) for c in range(4)] + [prv(c) for c in range(4)] + [nxt(c) for c in range(4)]
        + [const((1, 4 * W)), const((2, W)), const((2, 64, W)), const((2, W)), const((2, 64, W)),
           const((128, W)), const((1, W)), const((1, W)), const((1, W))],
        out_specs=[out] * 11,
        out_shape=[jax.ShapeDtypeStruct((n, W), F32)] * 11,
        compiler_params=_cparams(("arbitrary",)),
        name="rwprep",
    )(*([p] * 12), row(lp['rw_mu']), lp['rw_w0'], lp['rw_w2'], lp['rw_a0'], lp['rw_a2'], lp['rw_g2'],
      row(lp['rw_kk']), row(lp['rw_ka']), row(lp['rw_rk']))


def _rwscan_kernel(rf, vf, kkf, lwf, ktf, baf, rb, vb, kkb, lwb, ktb, bab, s0,
                   yf_o, yb_o, s_o, s_s, *, nc, nb):
    j = pl.program_id(1)
    T = RW_CHUNK

    @pl.when(j == 0)
    def _():
        s_s[...] = s0[...]

    row = lax.broadcasted_iota(jnp.int32, (T, T), 0)
    col = lax.broadcasted_iota(jnp.int32, (T, T), 1)
    incl = (row >= col, row <= col)
    strict = (row > col, row < col)
    refs = ((rf, vf, kkf, lwf, ktf, baf), (rb, vb, kkb, lwb, ktb, bab))
    grp = [(d, i) for d in range(2) for i in range(nb)]
    lws = [refs[d][3][i] for d, i in grp]
    lcs = [_sel_mm(incl[d].astype(BF16), lw) for (d, i), lw in zip(grp, lws)]
    units = []
    for (d, i), lw, lc in zip(grp, lws, lcs):
        r_ref, v_ref, kk_ref, _, kt_ref, ba_ref = refs[d]
        gam = jnp.exp(lc)
        ginv = jnp.exp(-lc)
        at = -kk_ref[i] * jnp.exp(lc - lw)
        ar = jnp.concatenate([at, r_ref[i] * gam], axis=0).astype(BF16)
        bk = jnp.concatenate([ba_ref[i] * ginv, kt_ref[i] * ginv], axis=0).astype(BF16)
        v = v_ref[i]
        vb16 = v.astype(BF16)
        gtot = gam[T - 1:T, :] if d == 0 else gam[0:1, :]
        for h in range(NH):
            sl = slice(h * HD, (h + 1) * HD)
            units.append(dict(d=d, i=i, u=d * NH + h, ar=ar[:, sl], bk=bk[:, sl], v=v[:, sl], vb=vb16[:, sl],
                              gtot=gtot[:, sl]))
    for un in units:
        un['S'] = s_s[un['i'], un['u']]
    for un in units:
        un['P'] = lax.dot_general(un['ar'], un['bk'], _NT, preferred_element_type=F32)
    for un in units:
        un['AS'] = _mm_nt(un['ar'], un['S'], False)
    for un in units:
        P, d = un['P'], un['d']
        un['A_ab'] = jnp.where(strict[d], P[0:T, 0:T], 0.0)
        un['A_rb'] = jnp.where(incl[d], P[T:2 * T, 0:T], 0.0).astype(BF16)
        un['A_k'] = jnp.concatenate([jnp.where(strict[d], P[0:T, T:2 * T], 0.0),
                                     jnp.where(incl[d], P[T:2 * T, T:2 * T], 0.0)], axis=0).astype(BF16)
    for un in units:
        un['AV'] = jnp.dot(un['A_k'], un['vb'], preferred_element_type=F32)
    pair = ((row >> 1) == (col >> 1)) & (row != col)
    eye = (row == col).astype(F32)
    for un in units:
        un['X'] = eye + jnp.where(pair, un['A_ab'], 0.0)
    bs = 2
    while bs < T:
        sh = bs.bit_length() - 1
        pair = ((row >> (sh + 1)) == (col >> (sh + 1))) & ((row >> sh) != (col >> sh))
        for un in units:
            un['Xb'] = un['X'].astype(BF16)
            un['XN'] = jnp.dot(un['Xb'], jnp.where(pair, un['A_ab'], 0.0).astype(BF16), preferred_element_type=F32)
        for un in units:
            un['X'] = un['X'] + jnp.dot(un['XN'].astype(BF16), un['Xb'], preferred_element_type=F32)
        bs *= 2
    for un in units:
        un['U'] = _mm(un['X'], un['AS'][0:T] + un['AV'][0:T], False)
    for un in units:
        un['Y'] = un['AS'][T:2 * T] + un['AV'][T:2 * T] + jnp.dot(un['A_rb'], un['U'].astype(BF16),
                                                                  preferred_element_type=F32)
    for un in units:
        UV = jnp.concatenate([un['U'], un['v']], axis=0)
        s_s[un['i'], un['u']] = (un['S'] + _mm(UV.T, un['bk'], False)) * un['gtot']
    ys = {}
    for un in units:
        ys.setdefault((un['d'], un['i']), []).append(un['Y'])
    for (d, i), parts in ys.items():
        (yf_o, yb_o)[d][i] = jnp.concatenate(parts, axis=1)

    @pl.when(j == nc - 1)
    def _():
        s_o[...] = s_s[...]


def _rwscan(prep, B, L, s0):
    T = RW_CHUNK
    nc = L // T
    nb = 2
    r, v, kk, lw0, lw1, kt0, kt1, ba0, ba1 = [a.reshape(B, L, W) for a in prep]
    fwd = pl.BlockSpec((nb, T, W), lambda b, j: (b, j, 0))
    bwd = pl.BlockSpec((nb, T, W), lambda b, j: (b, nc - 1 - j, 0))
    st = pl.BlockSpec((nb, 8, HD, HD), lambda b, j: (b, 0, 0, 0))
    yf, yb, S = pl.pallas_call(
        functools.partial(_rwscan_kernel, nc=nc, nb=nb),
        grid=(B // nb, nc),
        in_specs=[fwd] * 6 + [bwd] * 6 + [st],
        out_specs=[fwd, bwd, st],
        out_shape=[jax.ShapeDtypeStruct((B, L, W), F32), jax.ShapeDtypeStruct((B, L, W), F32),
                   jax.ShapeDtypeStruct((B, 8, HD, HD), F32)],
        scratch_shapes=[pltpu.VMEM((nb, 8, HD, HD), F32)],
        compiler_params=_cparams(("arbitrary", "arbitrary")),
        name="rwscan",
    )(r, v, kk, lw0, kt0, ba0, r, v, kk, lw1, kt1, ba1, s0)
    return yf.reshape(B * L, W), yb.reshape(B * L, W), S


def _rwfin_kernel(yf, yb, bon, g, lng, lnb, y_ref):
    y = _head_norm(yf[...] + yb[...], RW_GN_EPS) * lng[...] + lnb[...] + bon[...]
    y_ref[...] = y * g[...]


def _rwfin(yf, yb, bon, g, lng, lnb):
    n = yf.shape[0]
    tm = 512
    row = pl.BlockSpec((tm, W), lambda i: (i, 0))
    const = pl.BlockSpec((1, W), lambda i: (0, 0))
    return pl.pallas_call(
        _rwfin_kernel,
        grid=(n // tm,),
        in_specs=[row, row, row, row, const, const],
        out_specs=row,
        out_shape=jax.ShapeDtypeStruct((n, W), F32),
        compiler_params=_cparams(("arbitrary",)),
        name="rwfin",
    )(yf, yb, bon, g, lng, lnb)


def _rms(x, gain):
    ms = _mm_sel(x * x, _head_ones(x.shape[1])) * (1.0 / HD)
    return x * lax.rsqrt(ms + 1e-6) * gain


def _rope(x, cos, sin):
    wd = x.shape[1]
    lane = lax.broadcasted_iota(jnp.int32, x.shape, 1)
    partner = jnp.where((lane & 16) == 0, pltpu.roll(x, wd - 16, axis=1), pltpu.roll(x, 16, axis=1))
    return x * cos + partner * sin


def _attn_kernel(*refs, rope, cached, tq, kchunk):
    if rope:
        q_ref, k_ref, v_ref, qn, kn, cosq, sinq, cosk, sink = refs[:9]
        rest = refs[9:]
    else:
        q_ref, k_ref, v_ref, qn, kn = refs[:5]
        rest = refs[5:]
    if cached:
        kc_ref, vc_ref = rest[:2]
        rest = rest[2:]
    o_ref, kn_o, k_s = rest
    t = pl.program_id(1)

    @pl.when(t == 0)
    def _():
        k = _rms(k_ref[...], kn[...])
        if rope:
            k = _rope(k, cosk[...], sink[...])
        kn_o[...] = k
        k_s[...] = k.astype(BF16)

    q = _rms(q_ref[...], qn[...])
    if rope:
        q = _rope(q, cosq[...], sinq[...])
    q = (q * (HD ** -0.5)).astype(BF16)
    Lk = k_s.shape[0]
    for jkv in range(2):
        sl = slice(jkv * HD, (jkv + 1) * HD)
        q2 = jnp.concatenate([q[:, (2 * jkv) * HD:(2 * jkv + 1) * HD],
                              q[:, (2 * jkv + 1) * HD:(2 * jkv + 2) * HD]], axis=0)
        m = jnp.full((2 * tq, 1), -jnp.inf, F32)
        l = jnp.zeros((2 * tq, 1), F32)
        acc = jnp.zeros((2 * tq, HD), F32)
        chunks = [(k_s[c:c + kchunk, sl], v_ref[c:c + kchunk, sl].astype(BF16)) for c in range(0, Lk, kchunk)]
        if cached:
            chunks.append((kc_ref[0][:, sl].astype(BF16), vc_ref[0][:, sl].astype(BF16)))
        for kc, vc in chunks:
            s = _mm_nt(q2, kc, hi=False)
            m_new = jnp.maximum(m, jnp.max(s, axis=-1, keepdims=True))
            pr = jnp.exp(s - m_new)
            corr = jnp.exp(m - m_new)
            l = corr * l + jnp.sum(pr, axis=-1, keepdims=True)
            acc = corr * acc + jnp.dot(pr.astype(BF16), vc, preferred_element_type=F32)
            m = m_new
        o = acc / l
        o_ref[:, (2 * jkv) * HD:(2 * jkv + 1) * HD] = o[0:tq]
        o_ref[:, (2 * jkv + 1) * HD:(2 * jkv + 2) * HD] = o[tq:2 * tq]


def _attn(p, B, L, qn, kn, rope_tabs=None, cache=None):
    tq = min(L, 256)
    nt = L // tq
    kchunk = min(L, 1024)
    rope, cached = rope_tabs is not None, cache is not None
    cq, ck, cv = C_AT // W, (C_AT + W) // 128, (C_AT + W) // 128 + 1
    const = lambda shape: pl.BlockSpec(shape, lambda b, t: (0, 0))
    in_specs = [pl.BlockSpec((tq, W), lambda b, t: (b * nt + t, cq)),
                pl.BlockSpec((L, 128), lambda b, t: (b, ck)),
                pl.BlockSpec((L, 128), lambda b, t: (b, cv)),
                const((1, W)), const((1, 128))]
    args = [p, p, p, qn, kn]
    if rope:
        cos, sin = rope_tabs
        in_specs += [pl.BlockSpec((tq, W), lambda b, t: (t, 0)), pl.BlockSpec((tq, W), lambda b, t: (t, 0)),
                     const((L, 128)), const((L, 128))]
        args += [cos, sin, cos, sin]
    if cached:
        kc, vc = cache
        P = kc.shape[1]
        in_specs += [pl.BlockSpec((1, P, 128), lambda b, t: (b, 0, 0))] * 2
        args += [kc, vc]
    return pl.pallas_call(
        functools.partial(_attn_kernel, rope=rope, cached=cached, tq=tq, kchunk=kchunk),
        grid=(B, nt),
        in_specs=in_specs,
        out_specs=[pl.BlockSpec((tq, W), lambda b, t: (b * nt + t, 0)),
                   pl.BlockSpec((L, 128), lambda b, t: (b, 0))],
        out_shape=[jax.ShapeDtypeStruct((B * L, W), F32), jax.ShapeDtypeStruct((B * L, 128), F32)],
        scratch_shapes=[pltpu.VMEM((L, 128), BF16)],
        compiler_params=_cparams(("arbitrary", "arbitrary")),
        name="attn",
    )(*args)


def _rope_tables(L):
    t = jnp.arange(L)
    pos_r = (t // GRID_W).astype(F32)[:, None]
    pos_c = (t % GRID_W).astype(F32)[:, None]
    nf = HD // 4
    inv = ROPE_THETA ** (-jnp.arange(nf, dtype=F32) / nf)[None, :]
    ar, ac = pos_r * inv, pos_c * inv
    cos = jnp.concatenate([jnp.cos(ar), jnp.cos(ar), jnp.cos(ac), jnp.cos(ac)], axis=1)
    sin = jnp.concatenate([-jnp.sin(ar), jnp.sin(ar), -jnp.sin(ac), jnp.sin(ac)], axis=1)
    return jnp.tile(cos, (1, NH)), jnp.tile(sin, (1, NH))


def _layer(x, B, L, mod8, base, rows_per_cond, lp, consts, ctx):
    p = _inproj(x, mod8, lp['w_in_p'], rows_per_cond, base)
    y_hy = _hyena(p, B, L, consts['hy'], lp)

    if ctx is None:
        c0 = jnp.zeros((B, 8, HD, HD), F32)
        n0 = jnp.zeros((B, 8, HD), F32)
        m0 = jnp.zeros((B, 8, 128), F32)
        s0 = jnp.zeros((B, 8, HD, HD), F32)
        cache = None
    else:
        kc, vc, c0, n0, m0, s0 = ctx
        P = kc.shape[1]
        cache = (kc.reshape(B, P, 128), vc.reshape(B, P, 128))
        c0 = c0.reshape(B, 8, HD, HD)
        n0 = n0.reshape(B, 8, HD)
        m0 = jnp.broadcast_to(m0.reshape(B, 8, 1), (B, 8, 128))
        s0 = s0.reshape(B, 8, HD, HD)

    gbias = jnp.pad(lp['ml_gate_b'], (0, 128 - 4 * NH)).reshape(1, 128)
    hf, hb, C, n, m = _mlstm_scan(p, B, L, gbias, c0, n0, m0)
    y_ml = _mlfin(hf, hb, p, lp['ml_norm_g'].reshape(1, W))

    r, v, kk, g, bon, lw0, lw1, kt0, kt1, ba0, ba1 = _rwprep(p, B, L, lp)
    yf, yb, S = _rwscan((r, v, kk, lw0, lw1, kt0, kt1, ba0, ba1), B, L, s0)
    y_rw = _rwfin(yf, yb, bon, g, lp['rw_ln_g'].reshape(1, W), lp['rw_ln_b'].reshape(1, W))

    qn = jnp.tile(lp['at_qn'], NH).reshape(1, W)
    kn = jnp.tile(lp['at_kn'], 2).reshape(1, 128)
    y_at, k_new = _attn(p, B, L, qn, kn, consts.get('rope'), cache)
    v_new = p[:, C_AT + W + 128:C_AT + W + 256]

    row = lambda a: a.reshape(1, D)
    x = _outmlp((y_hy, y_ml, y_rw, y_at), x, mod8, lp['w_out_b'], row(lp['ln1_g']), row(lp['ln1_b']),
                lp['mlp_w1_b'], lp['mlp_w2_b'], row(lp['ln2_g']), row(lp['ln2_b']), rows_per_cond, base)
    state = (k_new.reshape(B, L, 2, HD), v_new.reshape(B, L, 2, HD), C.reshape(B, 2, NH, HD, HD),
             n.reshape(B, 2, NH, HD), m[:, :, 0].reshape(B, 2, NH), S.reshape(B, 2, NH, HD, HD))
    return x, state


def _permute_w_in(w_in):
    hy = w_in[:, :, 0:768]
    ml = w_in[:, :, 768:1792]
    mlg = w_in[:, :, 1792:1808]
    rw = w_in[:, :, 1808:2832]
    at = w_in[:, :, 2832:3344]
    mlg = jnp.pad(mlg, ((0, 0), (0, 0), (0, 128 - 16)))
    return jnp.concatenate([hy, ml, rw, at, mlg], axis=-1).astype(BF16)


def kernel(x_prompt, x_sample, cache_attn_k, cache_attn_v, state_mlstm_C, state_mlstm_n, state_mlstm_m, state_rwkv_S, c, c_ctx, w_mod, b_mod, w_in, hy_conv, hy_w1, hy_b1, hy_freq, hy_w2, hy_b2, hy_w3, hy_bias, ml_gate_b, ml_norm_g, rw_mu, rw_w0, rw_w2, rw_a0, rw_a2, rw_g2, rw_kk, rw_ka, rw_rk, rw_ln_g, rw_ln_b, at_qn, at_kn, w_out, ln1_g, ln1_b, mlp_w1, mlp_w2, ln2_g, ln2_b):
    Bp, Lp, _ = x_prompt.shape
    Bs, Ls, _ = x_sample.shape
    params = {
        'w_in_p': _permute_w_in(w_in), 'hy_conv': hy_conv, 'hy_w1': hy_w1, 'hy_b1': hy_b1, 'hy_freq': hy_freq,
        'hy_w2': hy_w2, 'hy_b2': hy_b2, 'hy_w3': hy_w3, 'hy_bias': hy_bias,
        'ml_gate_b': ml_gate_b, 'ml_norm_g': ml_norm_g,
        'rw_mu': rw_mu, 'rw_w0': rw_w0, 'rw_w2': rw_w2, 'rw_a0': rw_a0, 'rw_a2': rw_a2, 'rw_g2': rw_g2,
        'rw_kk': rw_kk, 'rw_ka': rw_ka, 'rw_rk': rw_rk, 'rw_ln_g': rw_ln_g, 'rw_ln_b': rw_ln_b,
        'at_qn': at_qn, 'at_kn': at_kn, 'w_out_b': w_out.astype(BF16), 'ln1_g': ln1_g, 'ln1_b': ln1_b,
        'mlp_w1_b': mlp_w1.astype(BF16), 'mlp_w2_b': mlp_w2.astype(BF16), 'ln2_g': ln2_g, 'ln2_b': ln2_b,
    }
    cvec = jnp.concatenate([c_ctx[None, :], c, jnp.zeros((8 - 1 - Bs, D), F32)], axis=0)
    mod = _mod_all(cvec, w_mod, b_mod).reshape(DEPTH, 8, 1, 6 * D)

    consts_p = {'hy': _hyena_consts(Lp) + _dft_consts(Lp)}
    consts_s = {'hy': _hyena_consts(Ls) + _dft_consts(Ls), 'rope': _rope_tables(Ls)}

    x = x_prompt.reshape(Bp * Lp, D)
    states = []
    for l in range(DEPTH):
        lp = {name: arr[l] for name, arr in params.items()}
        x, st = _layer(x, Bp, Lp, mod[l], 0, Bp * Lp, lp, consts_p, None)
        states.append(st)
    y_prompt = x.reshape(Bp, Lp, D)
    outs = [jnp.stack([st[i] for st in states], axis=1) for i in range(6)]

    x = x_sample.reshape(Bs * Ls, D)
    for l in range(DEPTH):
        lp = {name: arr[l] for name, arr in params.items()}
        ctx = (cache_attn_k[:, l], cache_attn_v[:, l], state_mlstm_C[:, l], state_mlstm_n[:, l],
               state_mlstm_m[:, l], state_rwkv_S[:, l])
        x, _ = _layer(x, Bs, Ls, mod[l], 1, Ls, lp, consts_s, ctx)
    y_sample = x.reshape(Bs, Ls, D)
    return (y_prompt, y_sample) + tuple(outs)
```

```python
import functools
import math

import jax
import jax.numpy as jnp
from jax import lax
from jax.experimental import pallas as pl
from jax.experimental.pallas import tpu as pltpu

F32 = jnp.float32
BF16 = jnp.bfloat16
HI = lax.Precision.HIGHEST

D = 1024
DEPTH = 4
W = 256
HD = 64
NH = 4
GRID_W = 64
D_FF = 4 * D
ALPHA = (2.0 * DEPTH) ** 0.25
HY_BANDS = 16
HY_EMB = 2 * HY_BANDS + 1
HY_FFN = 64
HY_MAX_DECAY = math.log(1e-2) / 0.3
HY_MIN_DECAY = math.log(1e-2) / 1.5
ML_CHUNK = 128
RW_CHUNK = 64
RW_DECAY_SCALE = 0.606531
RW_GN_EPS = 64e-5
ROPE_THETA = 10000.0

PC = 3456
C_HY = 0
C_ML = 768
C_RW = 1792
C_RWX = 2560
C_AT = 2816
C_MLG = 3328

VMEM_LIMIT = 56 * 1024 * 1024


def _cparams(sem):
    return pltpu.CompilerParams(dimension_semantics=sem, vmem_limit_bytes=VMEM_LIMIT)


_NN = (((1,), (0,)), ((), ()))
_NT = (((1,), (1,)), ((), ()))


def _dot(a, b, dn, hi):
    if hi:
        return lax.dot_general(a, b, dn, preferred_element_type=F32, precision=HI)
    return lax.dot_general(a.astype(BF16), b.astype(BF16), dn, preferred_element_type=F32)


def _mm(a, b, hi=True):
    return _dot(a, b, _NN, hi)


def _mm_nt(a, b, hi=True):
    return _dot(a, b, _NT, hi)


def _pieces(x, n):
    out = []
    for _ in range(n - 1):
        p = x.astype(BF16)
        out.append(p)
        x = x - p.astype(F32)
    out.append(x.astype(BF16))
    return out


def _sel_mm(sel, x, n=3):
    acc = None
    for p in _pieces(x, n):
        t = jnp.dot(sel, p, preferred_element_type=F32)
        acc = t if acc is None else acc + t
    return acc


def _mm_sel(x, sel, n=2):
    acc = None
    for p in _pieces(x, n):
        t = jnp.dot(p, sel, preferred_element_type=F32)
        acc = t if acc is None else acc + t
    return acc


def _head_ones(width):
    r = lax.broadcasted_iota(jnp.int32, (width, width), 0) >> 6
    c = lax.broadcasted_iota(jnp.int32, (width, width), 1) >> 6
    return (r == c).astype(BF16)


def _layernorm(x, g, b):
    mu = jnp.mean(x, -1, keepdims=True)
    xc = x - mu
    var = jnp.mean(xc * xc, -1, keepdims=True)
    return xc * lax.rsqrt(var + 1e-5) * g + b


def _head_norm(x, eps):
    ones = _head_ones(x.shape[1])
    mu = _mm_sel(x, ones) * (1.0 / HD)
    xc = x - mu
    var = _mm_sel(xc * xc, ones) * (1.0 / HD)
    return xc * lax.rsqrt(var + eps)


def _mod_kernel(c_ref, w_ref, b_ref, o_ref):
    c = c_ref[...]
    s = c * jax.nn.sigmoid(c)
    o_ref[0] = _mm(s, w_ref[0]) + b_ref[0]


def _mod_all(cvec, w_mod, b_mod):
    tn = 1536
    return pl.pallas_call(
        _mod_kernel,
        grid=(DEPTH, 6 * D // tn),
        in_specs=[pl.BlockSpec((8, D), lambda l, n: (0, 0)),
                  pl.BlockSpec((1, D, tn), lambda l, n: (l, 0, n)),
                  pl.BlockSpec((1, 1, tn), lambda l, n: (l, 0, n))],
        out_specs=pl.BlockSpec((1, 8, tn), lambda l, n: (l, 0, n)),
        out_shape=jax.ShapeDtypeStruct((DEPTH, 8, 6 * D), F32),
        compiler_params=_cparams(("arbitrary", "arbitrary")),
        name="mod",
    )(cvec, w_mod, b_mod.reshape(DEPTH, 1, 6 * D))


def _mod_row_map(rows_per_cond, tm, base):
    return lambda i: (base + (i * tm) // rows_per_cond, 0, 0)


def _inproj_kernel(x_ref, mod_ref, w_ref, o_ref):
    m = mod_ref[0]
    h = x_ref[...] * (1.0 + m[:, D:2 * D]) + m[:, 0:D]
    o_ref[...] = jnp.dot(h.astype(BF16), w_ref[...], preferred_element_type=F32)


def _inproj(x, mod8, w_in_p, rows_per_cond, base):
    n = x.shape[0]
    tm, tn = min(n, 1024), 1152
    rmap = _mod_row_map(rows_per_cond, tm, base)
    return pl.pallas_call(
        _inproj_kernel,
        grid=(PC // tn, n // tm),
        in_specs=[pl.BlockSpec((tm, D), lambda c, i: (i, 0)),
                  pl.BlockSpec((1, 1, 6 * D), lambda c, i: rmap(i)),
                  pl.BlockSpec((D, tn), lambda c, i: (0, c))],
        out_specs=pl.BlockSpec((tm, tn), lambda c, i: (i, c)),
        out_shape=jax.ShapeDtypeStruct((n, PC), F32),
        compiler_params=_cparams(("arbitrary", "arbitrary")),
        name="inproj",
    )(x, mod8, w_in_p)


def _outmlp_kernel(yh, hf, hb, og, mlg, yf, yb, bon, rg, lng, lnb, ya, x_ref, mod_ref,
                   wo, g1, b1, w1, w2, g2, b2, o_ref):
    m = mod_ref[0]
    y_ml = _head_norm(hf[...] + hb[...], 1e-6) * mlg[...] * jax.nn.sigmoid(og[...])
    y_rw = (_head_norm(yf[...] + yb[...], RW_GN_EPS) * lng[...] + lnb[...] + bon[...]) * rg[...]
    y = jnp.concatenate([yh[...], y_ml, y_rw, ya[...]], axis=-1).astype(BF16)
    mix = jnp.dot(y, wo[...], preferred_element_type=F32)
    x1 = _layernorm(ALPHA * x_ref[...] + m[:, 2 * D:3 * D] * mix, g1[...], b1[...])
    h = (x1 * (1.0 + m[:, 4 * D:5 * D]) + m[:, 3 * D:4 * D]).astype(BF16)
    acc = jnp.zeros(x1.shape, F32)
    for c in range(D_FF // D):
        a = jnp.dot(h, w1[:, c * D:(c + 1) * D], preferred_element_type=F32)
        a = jnp.square(jnp.maximum(a, 0.0)).astype(BF16)
        acc = acc + jnp.dot(a, w2[c * D:(c + 1) * D, :], preferred_element_type=F32)
    o_ref[...] = _layernorm(ALPHA * x1 + m[:, 5 * D:6 * D] * acc, g2[...], b2[...])


def _outmlp(ys, x, mod8, wo, g1, b1, w1, w2, g2, b2, rows_per_cond, base):
    n = x.shape[0]
    tm = 512
    rmap = _mod_row_map(rows_per_cond, tm, base)
    const = lambda shape: pl.BlockSpec(shape, lambda i: (0, 0), pipeline_mode=pl.Buffered(1))
    row = lambda wd: pl.BlockSpec((tm, wd), lambda i: (i, 0))
    vec = pl.BlockSpec((1, W), lambda i: (0, 0))
    return pl.pallas_call(
        _outmlp_kernel,
        grid=(n // tm,),
        in_specs=[row(W), row(W), row(W), pl.BlockSpec((tm, W), lambda i: (i, C_ML // W + 3)), vec,
                  row(W), row(W), row(W), row(W), vec, vec, row(W), row(D),
                  pl.BlockSpec((1, 1, 6 * D), lambda i: rmap(i)),
                  const((D, D)), const((1, D)), const((1, D)),
                  const((D, D_FF)), const((D_FF, D)), const((1, D)), const((1, D))],
        out_specs=row(D),
        out_shape=jax.ShapeDtypeStruct((n, D), F32),
        compiler_params=_cparams(("arbitrary",)),
        name="outmlp",
    )(*ys, x, mod8, wo, g1, b1, w1, w2, g2, b2)


def _hyfilt_kernel(z_ref, w1, b1, fr, w2, b2, w3, dec_ref, h_ref, nrm_ref):
    i = pl.program_id(0)
    f = fr[...]
    hid = jnp.sin(f * (_mm(z_ref[...], w1[...]) + b1[...]))
    hid = jnp.sin(f * (_mm(hid, w2[...]) + b2[...]))
    h = _mm(hid, w3[...])
    dec = dec_ref[...]
    h = h * jnp.concatenate([dec, dec, dec, dec], axis=1)
    rid = lax.broadcasted_iota(jnp.int32, h.shape, 0)
    cid = lax.broadcasted_iota(jnp.int32, h.shape, 1)
    anti = ((cid >> 8) & 1) == 1
    h = jnp.where(anti & (rid == 0) & (i == 0), 0.0, h)
    h_ref[...] = h.astype(BF16)
    part = jnp.broadcast_to(jnp.sum(jnp.abs(h), axis=0, keepdims=True), (8, 4 * W))

    @pl.when(i == 0)
    def _():
        nrm_ref[...] = part

    @pl.when(i > 0)
    def _():
        nrm_ref[...] = nrm_ref[...] + part


def _hyfilt(L, z, w1p, b1, fr, w2, b2, w3, dec):
    tl = min(L, 512)
    const = lambda shape: pl.BlockSpec(shape, lambda i: (0, 0))
    return pl.pallas_call(
        _hyfilt_kernel,
        grid=(L // tl,),
        in_specs=[pl.BlockSpec((tl, 128), lambda i: (i, 0)),
                  const((128, HY_FFN)), const((1, HY_FFN)), const((1, HY_FFN)),
                  const((HY_FFN, HY_FFN)), const((1, HY_FFN)), const((HY_FFN, 4 * W)),
                  pl.BlockSpec((tl, W), lambda i: (i, 0))],
        out_specs=[pl.BlockSpec((tl, 4 * W), lambda i: (i, 0)), const((8, 4 * W))],
        out_shape=[jax.ShapeDtypeStruct((L, 4 * W), BF16), jax.ShapeDtypeStruct((8, 4 * W), F32)],
        compiler_params=_cparams(("arbitrary",)),
        name="hyfilt",
    )(z, w1p, b1, fr, w2, b2, w3, dec)


def _hyspec_kernel(fc, fs, h_ref, nrm_ref, re_ref, im_ref, *, L):
    k = pl.program_id(0)
    h = h_ref[...]
    re = jnp.dot(fc[...], h, preferred_element_type=F32)
    im = jnp.dot(fs[...], h, preferred_element_type=F32)
    tk = re.shape[0]
    rid = lax.broadcasted_iota(jnp.int32, (tk, W), 0) + k * tk
    packed = rid == 0
    scale = jnp.where(packed, 1.0, 2.0) * (1.0 / (2 * L))
    nrm = nrm_ref[0:1, :]
    res, ims = [], []
    for f in range(2):
        c0, c1 = 2 * f * W, (2 * f + 1) * W
        inv = scale / (nrm[:, c0:c0 + W] + nrm[:, c1:c1 + W])
        res.append((re[:, c0:c0 + W] + re[:, c1:c1 + W]) * inv)
        ims.append(jnp.where(packed, im[:, c0:c0 + W] + im[:, c1:c1 + W],
                             im[:, c0:c0 + W] - im[:, c1:c1 + W]) * inv)
    re_ref[...] = jnp.concatenate(res, axis=1)
    im_ref[...] = jnp.concatenate(ims, axis=1)


def _hyspec(L, fc, fs, h, nrm):
    tk = min(L, 512)
    return pl.pallas_call(
        functools.partial(_hyspec_kernel, L=L),
        grid=(L // tk,),
        in_specs=[pl.BlockSpec((tk, L), lambda k: (k, 0)),
                  pl.BlockSpec((tk, L), lambda k: (k, 0)),
                  pl.BlockSpec((L, 4 * W), lambda k: (0, 0), pipeline_mode=pl.Buffered(1)),
                  pl.BlockSpec((8, 4 * W), lambda k: (0, 0))],
        out_specs=[pl.BlockSpec((tk, 2 * W), lambda k: (k, 0))] * 2,
        out_shape=[jax.ShapeDtypeStruct((L, 2 * W), F32)] * 2,
        compiler_params=_cparams(("arbitrary",)),
        name="hyspec",
    )(fc, fs, h, nrm)


def _hypre_kernel(x_ref, w_ref, o_ref):
    x = x_ref[...]
    L = x.shape[0]
    w = w_ref[0]
    rid = lax.broadcasted_iota(jnp.int32, x.shape, 0)
    xp = jnp.where(rid == 0, 0.0, pltpu.roll(x, 1, axis=0))
    xn = jnp.where(rid == L - 1, 0.0, pltpu.roll(x, L - 1, axis=0))
    o_ref[...] = xp * w[0:1, :] + x * w[1:2, :] + xn * w[2:3, :]


def _hypre(p, B, L, conv_w3):
    return pl.pallas_call(
        _hypre_kernel,
        grid=(B, 3),
        in_specs=[pl.BlockSpec((L, W), lambda b, c: (b, c)),
                  pl.BlockSpec((1, 3, W), lambda b, c: (c, 0, 0))],
        out_specs=pl.BlockSpec((L, W), lambda b, c: (b, c)),
        out_shape=jax.ShapeDtypeStruct((B * L, 3 * W), F32),
        compiler_params=_cparams(("arbitrary", "arbitrary")),
        name="hypre",
    )(p, conv_w3)


def _hyfwd_kernel(u_ref, fc, fs, fre_ref, fim_ref, yre_ref, yim_ref, ub):
    k = pl.program_id(1)

    @pl.when(k == 0)
    def _():
        ub[...] = u_ref[...].astype(BF16)

    u = ub[...]
    re = jnp.dot(fc[...], u, preferred_element_type=F32)
    im = jnp.dot(fs[...], u, preferred_element_type=F32)
    fre, fim = fre_ref[...], fim_ref[...]
    tk = re.shape[0]
    packed = (lax.broadcasted_iota(jnp.int32, re.shape, 0) + k * tk) == 0
    yre_ref[...] = jnp.where(packed, re * fre, re * fre - im * fim)
    yim_ref[...] = jnp.where(packed, im * fim, re * fim + im * fre)


def _hyfwd(u, ucol, B, L, fc, fs, fre, fim, f):
    tk = min(L, 512)
    return pl.pallas_call(
        _hyfwd_kernel,
        grid=(B, L // tk),
        in_specs=[pl.BlockSpec((L, W), lambda b, k: (b, ucol)),
                  pl.BlockSpec((tk, L), lambda b, k: (k, 0)),
                  pl.BlockSpec((tk, L), lambda b, k: (k, 0)),
                  pl.BlockSpec((tk, W), lambda b, k: (k, f)),
                  pl.BlockSpec((tk, W), lambda b, k: (k, f))],
        out_specs=[pl.BlockSpec((tk, W), lambda b, k: (b * (L // tk) + k, 0))] * 2,
        out_shape=[jax.ShapeDtypeStruct((B * L, W), F32)] * 2,
        scratch_shapes=[pltpu.VMEM((L, W), BF16)],
        compiler_params=_cparams(("arbitrary", "arbitrary")),
        name="hyfwd",
    )(u, fc, fs, fre, fim)


def _hyinv_kernel(yre_ref, yim_ref, fc, fst, u_ref, g_ref, bias_ref, o_ref, yb):
    t = pl.program_id(1)

    @pl.when(t == 0)
    def _():
        yb[0] = yre_ref[...].astype(BF16)
        yb[1] = yim_ref[...].astype(BF16)

    y = jnp.dot(fc[...], yb[0], preferred_element_type=F32) + jnp.dot(fst[...], yb[1], preferred_element_type=F32)
    o_ref[...] = g_ref[...] * (y + u_ref[...] * bias_ref[0])


def _hyinv(yre, yim, B, L, fc, fst, u, ucol, gate, gcol, bias, f):
    tt = min(L, 512)
    nt = L // tt
    return pl.pallas_call(
        _hyinv_kernel,
        grid=(B, nt),
        in_specs=[pl.BlockSpec((L, W), lambda b, t: (b, 0)),
                  pl.BlockSpec((L, W), lambda b, t: (b, 0)),
                  pl.BlockSpec((tt, L), lambda b, t: (t, 0)),
                  pl.BlockSpec((tt, L), lambda b, t: (t, 0)),
                  pl.BlockSpec((tt, W), lambda b, t: (b * nt + t, ucol)),
                  pl.BlockSpec((tt, W), lambda b, t: (b * nt + t, gcol)),
                  pl.BlockSpec((1, 1, W), lambda b, t: (f, 0, 0))],
        out_specs=pl.BlockSpec((tt, W), lambda b, t: (b * nt + t, 0)),
        out_shape=jax.ShapeDtypeStruct((B * L, W), F32),
        scratch_shapes=[pltpu.VMEM((2, L, W), BF16)],
        compiler_params=_cparams(("arbitrary", "arbitrary")),
        name="hyinv",
    )(yre, yim, fc, fst, u, gate, bias)


def _dft_consts(L):
    s = int(round(math.sqrt(L)))
    assert s * s == L
    k = jnp.arange(L, dtype=jnp.int32)[:, None]
    n1 = jnp.arange(s, dtype=jnp.int32)[None, :]
    a1 = ((k * (n1 * s)) % (2 * L)).astype(F32) * (math.pi / L)
    a2 = ((k * n1) % (2 * L)).astype(F32) * (math.pi / L)
    c1, s1, c2, s2 = jnp.cos(a1), jnp.sin(a1), jnp.cos(a2), jnp.sin(a2)
    fc = (c1[:, :, None] * c2[:, None, :] - s1[:, :, None] * s2[:, None, :]).reshape(L, L)
    fs = -(s1[:, :, None] * c2[:, None, :] + c1[:, :, None] * s2[:, None, :]).reshape(L, L)
    nyq = (1.0 - 2.0 * (jnp.arange(L) % 2)).astype(F32)[None, :]
    fs = jnp.where(k == 0, nyq, fs)
    return fc.astype(BF16), fs.astype(BF16), fs.T.astype(BF16)


def _hyena_consts(L):
    t = jnp.linspace(0.0, 1.0, L, dtype=F32)[:, None]
    ang = (2.0 * math.pi / L) * jnp.arange(L, dtype=F32)[:, None]
    bands = jnp.linspace(1e-4, HY_BANDS - 1, HY_BANDS, dtype=F32)[None, :]
    z = jnp.concatenate([t, jnp.cos(bands * ang), jnp.sin(bands * ang)], axis=-1)
    z = jnp.pad(z, ((0, 0), (0, 128 - HY_EMB)))
    deltas = jnp.abs(jnp.linspace(HY_MIN_DECAY, HY_MAX_DECAY, W, dtype=F32))
    dec = jnp.exp(-t * deltas)
    return z, dec


def _hyena(p, B, L, hc, lp):
    z, dec, fc, fs, fst = hc
    w1p = jnp.pad(lp['hy_w1'], ((0, 128 - HY_EMB), (0, 0)))
    row = lambda a: a.reshape(1, -1)
    h, nrm = _hyfilt(L, z, w1p, row(lp['hy_b1']), row(lp['hy_freq']), lp['hy_w2'], row(lp['hy_b2']),
                     lp['hy_w3'], dec)
    fre, fim = _hyspec(L, fc, fs, h, nrm)
    pc = _hypre(p, B, L, lp['hy_conv'].reshape(3, 3, W).transpose(1, 0, 2))
    bias = lp['hy_bias'].reshape(2, 1, W)
    yre, yim = _hyfwd(pc, 0, B, L, fc, fs, fre, fim, 0)
    zmid = _hyinv(yre, yim, B, L, fc, fst, pc, 0, pc, 1, bias, 0)
    yre, yim = _hyfwd(zmid, 0, B, L, fc, fs, fre, fim, 1)
    return _hyinv(yre, yim, B, L, fc, fst, zmid, 0, pc, 2, bias, 1)


def _mlstm_kernel(qf, kf, vf, gf, qb, kb, vb, gb, gbias, c0, n0, m0,
                  hf_o, hb_o, c_o, n_o, m_o, c_s, n_s, m_s, *, nc, nb):
    j = pl.program_id(1)
    T = ML_CHUNK

    @pl.when(j == 0)
    def _():
        c_s[...] = c0[...]
        n_s[...] = n0[...]
        m_s[...] = m0[...]

    lane = lax.broadcasted_iota(jnp.int32, (T, 128), 1)
    is_f = ((lane & 4) != 0) & (lane < 16)
    row = lax.broadcasted_iota(jnp.int32, (T, T), 0)
    col = lax.broadcasted_iota(jnp.int32, (T, T), 1)
    masks = (row >= col, row <= col)
    refs = ((qf, kf, vf, gf), (qb, kb, vb, gb))
    grp = [(d, i) for d in range(2) for i in range(nb)]
    Gs = []
    for d, i in grp:
        g = refs[d][3][i] + gbias[...]
        Gs.append(jnp.where(is_f, jnp.minimum(g, 0.0) - jnp.log1p(jnp.exp(-jnp.abs(g))), g))
    Bcs = [_sel_mm(masks[d].astype(BF16), G) for (d, i), G in zip(grp, Gs)]
    sr = lax.broadcasted_iota(jnp.int32, (128, NH * 128), 0)
    sc = lax.broadcasted_iota(jnp.int32, (128, NH * 128), 1) >> 7
    units = []
    for gi, (d, i) in enumerate(grp):
        G, Bc = Gs[gi], Bcs[gi]
        GT, BcT = G.T, Bc.T
        Bcol = _mm_sel(Bc, (sr == sc + (d * 8 + 4)).astype(BF16), 3)
        Icol = _mm_sel(G, (sr == sc + d * 8).astype(BF16), 3)
        q32 = refs[d][0][i] * (HD ** -0.5)
        q = q32.astype(BF16)
        k = refs[d][1][i]
        v = refs[d][2][i].astype(BF16)
        for h in range(NH):
            cf, ci = d * 8 + 4 + h, d * 8 + h
            sl = slice(h * HD, (h + 1) * HD)
            b_col = Bcol[:, h * 128:(h + 1) * 128]
            units.append(dict(
                d=d, i=i, u=d * NH + h, q=q[:, sl], q32=q32[:, sl], k=k[:, sl], v=v[:, sl],
                b_col=b_col, b_row=BcT[cf:cf + 1, :], i_col=Icol[:, h * 128:(h + 1) * 128], i_row=GT[ci:ci + 1, :],
                b_tot=b_col[T - 1:T, :] if d == 0 else b_col[0:1, :]))
    for un in units:
        un['m_prev'] = m_s[un['i'], un['u']:un['u'] + 1, :]
        un['C'] = c_s[un['i'], un['u']]
        un['n'] = n_s[un['i'], un['u']:un['u'] + 1, :]
    for un in units:
        un['qk'] = _mm_nt(un['q'], un['k'], False)
    for un in units:
        un['qC'] = _mm(un['q'], un['C'], False)
    for un in units:
        logd = jnp.where(masks[un['d']], un['b_col'] - un['b_row'] + un['i_row'], -jnp.inf)
        inter = un['b_col'] + un['m_prev']
        m_t = jnp.maximum(inter, jnp.max(logd, axis=-1, keepdims=True))
        un['s'] = un['qk'] * jnp.exp(logd - m_t)
        un['w_in'] = jnp.exp(inter - m_t)
        un['floor'] = jnp.exp(-m_t)
        logw = un['b_tot'] - un['b_col'] + un['i_col']
        m_new = jnp.maximum(un['b_tot'] + un['m_prev'], jnp.max(logw, axis=0, keepdims=True))
        un['kw'] = un['k'] * jnp.exp(logw - m_new)[:, 0:HD]
        un['decay'] = jnp.exp(un['b_tot'] + un['m_prev'] - m_new)[:, 0:HD]
        un['m_new'] = m_new
    for un in units:
        un['sv'] = _mm(un['s'], un['v'], False)
    for un in units:
        un['kv'] = _mm(un['kw'].T, un['v'], False)
    hs = {}
    for un in units:
        i, u = un['i'], un['u']
        num = un['sv'] + un['w_in'][:, 0:HD] * un['qC']
        den = jnp.sum(un['s'], axis=-1, keepdims=True) \
            + un['w_in'] * jnp.sum(un['q32'] * un['n'], axis=-1, keepdims=True)
        hs.setdefault((un['d'], i), []).append(num / jnp.maximum(jnp.abs(den), un['floor'])[:, 0:HD])
        c_s[i, u] = un['decay'] * un['C'] + un['kv']
        n_s[i, u:u + 1, :] = un['decay'] * un['n'] + jnp.sum(un['kw'], axis=0, keepdims=True)
        m_s[i, u:u + 1, :] = un['m_new']
    for (d, i), parts in hs.items():
        (hf_o, hb_o)[d][i] = jnp.concatenate(parts, axis=1)

    @pl.when(j == nc - 1)
    def _():
        c_o[...] = c_s[...]
        n_o[...] = n_s[...]
        m_o[...] = m_s[...]


def _mlstm_scan(p, B, L, gbias, c0, n0, m0):
    T = ML_CHUNK
    nc = L // T
    nb = 2
    p3 = p.reshape(B, L, PC)
    fwd = lambda cb: (lambda b, j: (b, j, cb))
    bwd = lambda cb: (lambda b, j: (b, nc - 1 - j, cb))
    blk = lambda f, cb, wd=W: pl.BlockSpec((nb, T, wd), f(cb))
    cq, ck, cv, cg = C_ML // W, C_ML // W + 1, C_ML // W + 2, C_MLG // 128
    st = lambda shape: pl.BlockSpec((nb,) + shape, lambda b, j: (b,) + (0,) * len(shape))
    hf, hb, C, n, m = pl.pallas_call(
        functools.partial(_mlstm_kernel, nc=nc, nb=nb),
        grid=(B // nb, nc),
        in_specs=[blk(fwd, cq), blk(fwd, ck), blk(fwd, cv), blk(fwd, cg, 128),
                  blk(bwd, cq), blk(bwd, ck), blk(bwd, cv), blk(bwd, cg, 128),
                  pl.BlockSpec((1, 128), lambda b, j: (0, 0)),
                  st((8, HD, HD)), st((8, HD)), st((8, 128))],
        out_specs=[pl.BlockSpec((nb, T, W), lambda b, j: (b, j, 0)),
                   pl.BlockSpec((nb, T, W), lambda b, j: (b, nc - 1 - j, 0)),
                   st((8, HD, HD)), st((8, HD)), st((8, 128))],
        out_shape=[jax.ShapeDtypeStruct((B, L, W), F32), jax.ShapeDtypeStruct((B, L, W), F32),
                   jax.ShapeDtypeStruct((B, 8, HD, HD), F32), jax.ShapeDtypeStruct((B, 8, HD), F32),
                   jax.ShapeDtypeStruct((B, 8, 128), F32)],
        scratch_shapes=[pltpu.VMEM((nb, 8, HD, HD), F32), pltpu.VMEM((nb, 8, HD), F32),
                        pltpu.VMEM((nb, 8, 128), F32)],
        compiler_params=_cparams(("arbitrary", "arbitrary")),
        name="mlstm",
    )(p3, p3, p3, p3, p3, p3, p3, p3, gbias, c0, n0, m0)
    return hf.reshape(B * L, W), hb.reshape(B * L, W), C, n, m


def _shifted(x, prev8, next8, first, last):
    T = x.shape[0]
    rid = lax.broadcasted_iota(jnp.int32, x.shape, 0)
    pr = jnp.where(first, 0.0, prev8[7:8, :])
    nx = jnp.where(last, 0.0, next8[0:1, :])
    xp = jnp.where(rid == 0, pr, pltpu.roll(x, 1, axis=0))
    xn = jnp.where(rid == T - 1, nx, pltpu.roll(x, T - 1, axis=0))
    return xp, xn


def _rwprep_kernel(*refs, L, T):
    xs, prevs, nexts = refs[0:4], refs[4:8], refs[8:12]
    mu, w0, w2, a0, a2, g2, kkw, kaw, rkw = refs[12:21]
    r_o, v_o, kk_o, g_o, bon_o, lw0_o, lw1_o, kt0_o, kt1_o, ba0_o, ba1_o = refs[21:]
    i = pl.program_id(0)
    first = (i * T) % L == 0
    last = ((i + 1) * T) % L == 0
    pf = []
    for c in range(4):
        x = xs[c][...]
        xp, xn = _shifted(x, prevs[c][...], nexts[c][...], first, last)
        pf.append(x + mu[:, c * W:(c + 1) * W] * (0.5 * (xp + xn) - x))
    r, k, v, misc = pf
    lw, la, lg = misc[:, 0:64], misc[:, 64:128], misc[:, 128:256]
    ones = _head_ones(W)
    g_o[...] = _mm(jax.nn.sigmoid(lg), g2[...])
    kq = k * kkw[...]
    kk = kq / jnp.maximum(jnp.sqrt(_mm_sel(kq * kq, ones)), 1e-12)
    tw = jnp.tanh(lw)
    r_o[...] = r
    v_o[...] = v
    kk_o[...] = kk
    bon = jnp.zeros_like(r)
    for d, (lw_o, kt_o, ba_o) in enumerate(((lw0_o, kt0_o, ba0_o), (lw1_o, kt1_o, ba1_o))):
        lw_o[...] = -RW_DECAY_SCALE * jax.nn.sigmoid(w0[d:d + 1, :] + _mm(tw, w2[d]))
        a = jax.nn.sigmoid(a0[d:d + 1, :] + _mm(la, a2[d]))
        kt = k * (1.0 + (a - 1.0) * kaw[...])
        kt_o[...] = kt
        ba_o[...] = kk * a
        bon = bon + _mm_sel(r * kt * rkw[...], ones) * v
    bon_o[...] = bon


def _rwprep(p, B, L, lp):
    n = B * L
    T = min(L, 512)
    nb8 = n // 8
    c0 = C_RW // W
    cur = lambda c: pl.BlockSpec((T, W), lambda i: (i, c0 + c))
    prv = lambda c: pl.BlockSpec((8, W), lambda i: (jnp.maximum(i * (T // 8) - 1, 0), c0 + c))
    nxt = lambda c: pl.BlockSpec((8, W), lambda i: (jnp.minimum((i + 1) * (T // 8), nb8 - 1), c0 + c))
    const = lambda shape: pl.BlockSpec(shape, lambda i: (0,) * len(shape))
    row = lambda a: a.reshape(1, -1)
    out = pl.BlockSpec((T, W), lambda i: (i, 0))
    return pl.pallas_call(
        functools.partial(_rwprep_kernel, L=L, T=T),
        grid=(n // T,),
        in_specs=[cur(c) for c in range(4)] + [prv(c) for c in range(4)] + [nxt(c) for c in range(4)]
        + [const((1, 4 * W)), const((2, W)), const((2, 64, W)), const((2, W)), const((2, 64, W)),
           const((128, W)), const((1, W)), const((1, W)), const((1, W))],
        out_specs=[out] * 11,
        out_shape=[jax.ShapeDtypeStruct((n, W), F32)] * 11,
        compiler_params=_cparams(("arbitrary",)),
        name="rwprep",
    )(*([p] * 12), row(lp['rw_mu']), lp['rw_w0'], lp['rw_w2'], lp['rw_a0'], lp['rw_a2'], lp['rw_g2'],
      row(lp['rw_kk']), row(lp['rw_ka']), row(lp['rw_rk']))


def _rwscan_kernel(rf, vf, kkf, lwf, ktf, baf, rb, vb, kkb, lwb, ktb, bab, s0,
                   yf_o, yb_o, s_o, s_s, *, nc, nb):
    j = pl.program_id(1)
    T = RW_CHUNK

    @pl.when(j == 0)
    def _():
        s_s[...] = s0[...]

    row = lax.broadcasted_iota(jnp.int32, (T, T), 0)
    col = lax.broadcasted_iota(jnp.int32, (T, T), 1)
    incl = (row >= col, row <= col)
    strict = (row > col, row < col)
    refs = ((rf, vf, kkf, lwf, ktf, baf), (rb, vb, kkb, lwb, ktb, bab))
    grp = [(d, i) for d in range(2) for i in range(nb)]
    lws = [refs[d][3][i] for d, i in grp]
    lcs = [_sel_mm(incl[d].astype(BF16), lw) for (d, i), lw in zip(grp, lws)]
    units = []
    for (d, i), lw, lc in zip(grp, lws, lcs):
        r_ref, v_ref, kk_ref, _, kt_ref, ba_ref = refs[d]
        gam = jnp.exp(lc)
        ginv = jnp.exp(-lc)
        at = -kk_ref[i] * jnp.exp(lc - lw)
        ar = jnp.concatenate([at, r_ref[i] * gam], axis=0).astype(BF16)
        bk = jnp.concatenate([ba_ref[i] * ginv, kt_ref[i] * ginv], axis=0).astype(BF16)
        v = v_ref[i]
        vb16 = v.astype(BF16)
        gtot = gam[T - 1:T, :] if d == 0 else gam[0:1, :]
        for h in range(NH):
            sl = slice(h * HD, (h + 1) * HD)
            units.append(dict(d=d, i=i, u=d * NH + h, ar=ar[:, sl], bk=bk[:, sl], v=v[:, sl], vb=vb16[:, sl],
                              gtot=gtot[:, sl]))
    for un in units:
        un['S'] = s_s[un['i'], un['u']]
    for un in units:
        un['P'] = lax.dot_general(un['ar'], un['bk'], _NT, preferred_element_type=F32)
    for un in units:
        un['AS'] = _mm_nt(un['ar'], un['S'], False)
    for un in units:
        P, d = un['P'], un['d']
        un['A_ab'] = jnp.where(strict[d], P[0:T, 0:T], 0.0)
        un['A_rb'] = jnp.where(incl[d], P[T:2 * T, 0:T], 0.0).astype(BF16)
        un['A_k'] = jnp.concatenate([jnp.where(strict[d], P[0:T, T:2 * T], 0.0),
                                     jnp.where(incl[d], P[T:2 * T, T:2 * T], 0.0)], axis=0).astype(BF16)
    for un in units:
        un['AV'] = jnp.dot(un['A_k'], un['vb'], preferred_element_type=F32)
    pair = ((row >> 1) == (col >> 1)) & (row != col)
    eye = (row == col).astype(F32)
    for un in units:
        un['X'] = eye + jnp.where(pair, un['A_ab'], 0.0)
    bs = 2
    while bs < T:
        sh = bs.bit_length() - 1
        pair = ((row >> (sh + 1)) == (col >> (sh + 1))) & ((row >> sh) != (col >> sh))
        for un in units:
            un['Xb'] = un['X'].astype(BF16)
            un['XN'] = jnp.dot(un['Xb'], jnp.where(pair, un['A_ab'], 0.0).astype(BF16), preferred_element_type=F32)
        for un in units:
            un['X'] = un['X'] + jnp.dot(un['XN'].astype(BF16), un['Xb'], preferred_element_type=F32)
        bs *= 2
    for un in units:
        un['U'] = _mm(un['X'], un['AS'][0:T] + un['AV'][0:T], False)
    for un in units:
        un['Y'] = un['AS'][T:2 * T] + un['AV'][T:2 * T] + jnp.dot(un['A_rb'], un['U'].astype(BF16),
                                                                  preferred_element_type=F32)
    for un in units:
        UV = jnp.concatenate([un['U'], un['v']], axis=0)
        s_s[un['i'], un['u']] = (un['S'] + _mm(UV.T, un['bk'], False)) * un['gtot']
    ys = {}
    for un in units:
        ys.setdefault((un['d'], un['i']), []).append(un['Y'])
    for (d, i), parts in ys.items():
        (yf_o, yb_o)[d][i] = jnp.concatenate(parts, axis=1)

    @pl.when(j == nc - 1)
    def _():
        s_o[...] = s_s[...]


def _rwscan(prep, B, L, s0):
    T = RW_CHUNK
    nc = L // T
    nb = 2
    r, v, kk, lw0, lw1, kt0, kt1, ba0, ba1 = [a.reshape(B, L, W) for a in prep]
    fwd = pl.BlockSpec((nb, T, W), lambda b, j: (b, j, 0))
    bwd = pl.BlockSpec((nb, T, W), lambda b, j: (b, nc - 1 - j, 0))
    st = pl.BlockSpec((nb, 8, HD, HD), lambda b, j: (b, 0, 0, 0))
    yf, yb, S = pl.pallas_call(
        functools.partial(_rwscan_kernel, nc=nc, nb=nb),
        grid=(B // nb, nc),
        in_specs=[fwd] * 6 + [bwd] * 6 + [st],
        out_specs=[fwd, bwd, st],
        out_shape=[jax.ShapeDtypeStruct((B, L, W), F32), jax.ShapeDtypeStruct((B, L, W), F32),
                   jax.ShapeDtypeStruct((B, 8, HD, HD), F32)],
        scratch_shapes=[pltpu.VMEM((nb, 8, HD, HD), F32)],
        compiler_params=_cparams(("arbitrary", "arbitrary")),
        name="rwscan",
    )(r, v, kk, lw0, kt0, ba0, r, v, kk, lw1, kt1, ba1, s0)
    return yf.reshape(B * L, W), yb.reshape(B * L, W), S


def _rms(x, gain):
    ms = _mm_sel(x * x, _head_ones(x.shape[1])) * (1.0 / HD)
    return x * lax.rsqrt(ms + 1e-6) * gain


def _rope(x, cos, sin):
    wd = x.shape[1]
    lane = lax.broadcasted_iota(jnp.int32, x.shape, 1)
    partner = jnp.where((lane & 16) == 0, pltpu.roll(x, wd - 16, axis=1), pltpu.roll(x, 16, axis=1))
    return x * cos + partner * sin


def _attn_kernel(*refs, rope, cached, tq, kchunk):
    if rope:
        q_ref, k_ref, v_ref, qn, kn, cosq, sinq, cosk, sink = refs[:9]
        rest = refs[9:]
    else:
        q_ref, k_ref, v_ref, qn, kn = refs[:5]
        rest = refs[5:]
    if cached:
        kc_ref, vc_ref = rest[:2]
        rest = rest[2:]
    o_ref, kn_o, k_s = rest
    t = pl.program_id(1)

    @pl.when(t == 0)
    def _():
        k = _rms(k_ref[...], kn[...])
        if rope:
            k = _rope(k, cosk[...], sink[...])
        kn_o[...] = k
        k_s[...] = k.astype(BF16)

    q = _rms(q_ref[...], qn[...])
    if rope:
        q = _rope(q, cosq[...], sinq[...])
    q = (q * (HD ** -0.5)).astype(BF16)
    Lk = k_s.shape[0]
    for jkv in range(2):
        sl = slice(jkv * HD, (jkv + 1) * HD)
        q2 = jnp.concatenate([q[:, (2 * jkv) * HD:(2 * jkv + 1) * HD],
                              q[:, (2 * jkv + 1) * HD:(2 * jkv + 2) * HD]], axis=0)
        m = jnp.full((2 * tq, 1), -jnp.inf, F32)
        l = jnp.zeros((2 * tq, 1), F32)
        acc = jnp.zeros((2 * tq, HD), F32)
        chunks = [(k_s[c:c + kchunk, sl], v_ref[c:c + kchunk, sl].astype(BF16)) for c in range(0, Lk, kchunk)]
        if cached:
            chunks.append((kc_ref[0][:, sl].astype(BF16), vc_ref[0][:, sl].astype(BF16)))
        for kc, vc in chunks:
            s = _mm_nt(q2, kc, hi=False)
            m_new = jnp.maximum(m, jnp.max(s, axis=-1, keepdims=True))
            pr = jnp.exp(s - m_new)
            corr = jnp.exp(m - m_new)
            l = corr * l + jnp.sum(pr, axis=-1, keepdims=True)
            acc = corr * acc + jnp.dot(pr.astype(BF16), vc, preferred_element_type=F32)
            m = m_new
        o = acc / l
        o_ref[:, (2 * jkv) * HD:(2 * jkv + 1) * HD] = o[0:tq]
        o_ref[:, (2 * jkv + 1) * HD:(2 * jkv + 2) * HD] = o[tq:2 * tq]


def _attn(p, B, L, qn, kn, rope_tabs=None, cache=None):
    tq = min(L, 256)
    nt = L // tq
    kchunk = min(L, 1024)
    rope, cached = rope_tabs is not None, cache is not None
    cq, ck, cv = C_AT // W, (C_AT + W) // 128, (C_AT + W) // 128 + 1
    const = lambda shape: pl.BlockSpec(shape, lambda b, t: (0, 0))
    in_specs = [pl.BlockSpec((tq, W), lambda b, t: (b * nt + t, cq)),
                pl.BlockSpec((L, 128), lambda b, t: (b, ck)),
                pl.BlockSpec((L, 128), lambda b, t: (b, cv)),
                const((1, W)), const((1, 128))]
    args = [p, p, p, qn, kn]
    if rope:
        cos, sin = rope_tabs
        in_specs += [pl.BlockSpec((tq, W), lambda b, t: (t, 0)), pl.BlockSpec((tq, W), lambda b, t: (t, 0)),
                     const((L, 128)), const((L, 128))]
        args += [cos, sin, cos, sin]
    if cached:
        kc, vc = cache
        P = kc.shape[1]
        in_specs += [pl.BlockSpec((1, P, 128), lambda b, t: (b, 0, 0))] * 2
        args += [kc, vc]
    return pl.pallas_call(
        functools.partial(_attn_kernel, rope=rope, cached=cached, tq=tq, kchunk=kchunk),
        grid=(B, nt),
        in_specs=in_specs,
        out_specs=[pl.BlockSpec((tq, W), lambda b, t: (b * nt + t, 0)),
                   pl.BlockSpec((L, 128), lambda b, t: (b, 0))],
        out_shape=[jax.ShapeDtypeStruct((B * L, W), F32), jax.ShapeDtypeStruct((B * L, 128), F32)],
        scratch_shapes=[pltpu.VMEM((L, 128), BF16)],
        compiler_params=_cparams(("arbitrary", "arbitrary")),
        name="attn",
    )(*args)


def _rope_tables(L):
    t = jnp.arange(L)
    pos_r = (t // GRID_W).astype(F32)[:, None]
    pos_c = (t % GRID_W).astype(F32)[:, None]
    nf = HD // 4
    inv = ROPE_THETA ** (-jnp.arange(nf, dtype=F32) / nf)[None, :]
    ar, ac = pos_r * inv, pos_c * inv
    cos = jnp.concatenate([jnp.cos(ar), jnp.cos(ar), jnp.cos(ac), jnp.cos(ac)], axis=1)
    sin = jnp.concatenate([-jnp.sin(ar), jnp.sin(ar), -jnp.sin(ac), jnp.sin(ac)], axis=1)
    return jnp.tile(cos, (1, NH)), jnp.tile(sin, (1, NH))


def _layer(x, B, L, mod8, base, rows_per_cond, lp, consts, ctx):
    p = _inproj(x, mod8, lp['w_in_p'], rows_per_cond, base)
    y_hy = _hyena(p, B, L, consts['hy'], lp)

    if ctx is None:
        c0 = jnp.zeros((B, 8, HD, HD), F32)
        n0 = jnp.zeros((B, 8, HD), F32)
        m0 = jnp.zeros((B, 8, 128), F32)
        s0 = jnp.zeros((B, 8, HD, HD), F32)
        cache = None
    else:
        kc, vc, c0, n0, m0, s0 = ctx
        P = kc.shape[1]
        cache = (kc.reshape(B, P, 128), vc.reshape(B, P, 128))
        c0 = c0.reshape(B, 8, HD, HD)
        n0 = n0.reshape(B, 8, HD)
        m0 = jnp.broadcast_to(m0.reshape(B, 8, 1), (B, 8, 128))
        s0 = s0.reshape(B, 8, HD, HD)

    gbias = jnp.pad(lp['ml_gate_b'], (0, 128 - 4 * NH)).reshape(1, 128)
    hf, hb, C, n, m = _mlstm_scan(p, B, L, gbias, c0, n0, m0)

    r, v, kk, g, bon, lw0, lw1, kt0, kt1, ba0, ba1 = _rwprep(p, B, L, lp)
    yf, yb, S = _rwscan((r, v, kk, lw0, lw1, kt0, kt1, ba0, ba1), B, L, s0)

    qn = jnp.tile(lp['at_qn'], NH).reshape(1, W)
    kn = jnp.tile(lp['at_kn'], 2).reshape(1, 128)
    y_at, k_new = _attn(p, B, L, qn, kn, consts.get('rope'), cache)
    v_new = p[:, C_AT + W + 128:C_AT + W + 256]

    row = lambda a: a.reshape(1, D)
    vec = lambda a: a.reshape(1, W)
    ys = (y_hy, hf, hb, p, vec(lp['ml_norm_g']), yf, yb, bon, g, vec(lp['rw_ln_g']), vec(lp['rw_ln_b']), y_at)
    x = _outmlp(ys, x, mod8, lp['w_out_b'], row(lp['ln1_g']), row(lp['ln1_b']),
                lp['mlp_w1_b'], lp['mlp_w2_b'], row(lp['ln2_g']), row(lp['ln2_b']), rows_per_cond, base)
    state = (k_new.reshape(B, L, 2, HD), v_new.reshape(B, L, 2, HD), C.reshape(B, 2, NH, HD, HD),
             n.reshape(B, 2, NH, HD), m[:, :, 0].reshape(B, 2, NH), S.reshape(B, 2, NH, HD, HD))
    return x, state


def _permute_w_in(w_in):
    hy = w_in[:, :, 0:768]
    ml = w_in[:, :, 768:1792]
    mlg = w_in[:, :, 1792:1808]
    rw = w_in[:, :, 1808:2832]
    at = w_in[:, :, 2832:3344]
    mlg = jnp.pad(mlg, ((0, 0), (0, 0), (0, 128 - 16)))
    return jnp.concatenate([hy, ml, rw, at, mlg], axis=-1).astype(BF16)


def kernel(x_prompt, x_sample, cache_attn_k, cache_attn_v, state_mlstm_C, state_mlstm_n, state_mlstm_m, state_rwkv_S, c, c_ctx, w_mod, b_mod, w_in, hy_conv, hy_w1, hy_b1, hy_freq, hy_w2, hy_b2, hy_w3, hy_bias, ml_gate_b, ml_norm_g, rw_mu, rw_w0, rw_w2, rw_a0, rw_a2, rw_g2, rw_kk, rw_ka, rw_rk, rw_ln_g, rw_ln_b, at_qn, at_kn, w_out, ln1_g, ln1_b, mlp_w1, mlp_w2, ln2_g, ln2_b):
    Bp, Lp, _ = x_prompt.shape
    Bs, Ls, _ = x_sample.shape
    params = {
        'w_in_p': _permute_w_in(w_in), 'hy_conv': hy_conv, 'hy_w1': hy_w1, 'hy_b1': hy_b1, 'hy_freq': hy_freq,
        'hy_w2': hy_w2, 'hy_b2': hy_b2, 'hy_w3': hy_w3, 'hy_bias': hy_bias,
        'ml_gate_b': ml_gate_b, 'ml_norm_g': ml_norm_g,
        'rw_mu': rw_mu, 'rw_w0': rw_w0, 'rw_w2': rw_w2, 'rw_a0': rw_a0, 'rw_a2': rw_a2, 'rw_g2': rw_g2,
        'rw_kk': rw_kk, 'rw_ka': rw_ka, 'rw_rk': rw_rk, 'rw_ln_g': rw_ln_g, 'rw_ln_b': rw_ln_b,
        'at_qn': at_qn, 'at_kn': at_kn, 'w_out_b': w_out.astype(BF16), 'ln1_g': ln1_g, 'ln1_b': ln1_b,
        'mlp_w1_b': mlp_w1.astype(BF16), 'mlp_w2_b': mlp_w2.astype(BF16), 'ln2_g': ln2_g, 'ln2_b': ln2_b,
    }
    cvec = jnp.concatenate([c_ctx[None, :], c, jnp.zeros((8 - 1 - Bs, D), F32)], axis=0)
    mod = _mod_all(cvec, w_mod, b_mod).reshape(DEPTH, 8, 1, 6 * D)

    consts_p = {'hy': _hyena_consts(Lp) + _dft_consts(Lp)}
    consts_s = {'hy': _hyena_consts(Ls) + _dft_consts(Ls), 'rope': _rope_tables(Ls)}

    x = x_prompt.reshape(Bp * Lp, D)
    states = []
    for l in range(DEPTH):
        lp = {name: arr[l] for name, arr in params.items()}
        x, st = _layer(x, Bp, Lp, mod[l], 0, Bp * Lp, lp, consts_p, None)
        states.append(st)
    y_prompt = x.reshape(Bp, Lp, D)
    outs = [jnp.stack([st[i] for st in states], axis=1) for i in range(6)]

    x = x_sample.reshape(Bs * Ls, D)
    for l in range(DEPTH):
        lp = {name: arr[l] for name, arr in params.items()}
        ctx = (cache_attn_k[:, l], cache_attn_v[:, l], state_mlstm_C[:, l], state_mlstm_n[:, l],
               state_mlstm_m[:, l], state_rwkv_S[:, l])
        x, _ = _layer(x, Bs, Ls, mod[l], 1, Ls, lp, consts_s, ctx)
    y_sample = x.reshape(Bs, Ls, D)
    return (y_prompt, y_sample) + tuple(outs)
```

```python
import functools
import math

import jax
import jax.numpy as jnp
from jax import lax
from jax.experimental import pallas as pl
from jax.experimental.pallas import tpu as pltpu

F32 = jnp.float32
BF16 = jnp.bfloat16
HI = lax.Precision.HIGHEST

D = 1024
DEPTH = 4
W = 256
HD = 64
NH = 4
GRID_W = 64
D_FF = 4 * D
ALPHA = (2.0 * DEPTH) ** 0.25
HY_BANDS = 16
HY_EMB = 2 * HY_BANDS + 1
HY_FFN = 64
HY_MAX_DECAY = math.log(1e-2) / 0.3
HY_MIN_DECAY = math.log(1e-2) / 1.5
ML_CHUNK = 128
RW_CHUNK = 64
RW_DECAY_SCALE = 0.606531
RW_GN_EPS = 64e-5
ROPE_THETA = 10000.0

PC = 3456
C_HY = 0
C_ML = 768
C_RW = 1792
C_RWX = 2560
C_AT = 2816
C_MLG = 3328

VMEM_LIMIT = 56 * 1024 * 1024


def _cparams(sem):
    return pltpu.CompilerParams(dimension_semantics=sem, vmem_limit_bytes=VMEM_LIMIT)


_NN = (((1,), (0,)), ((), ()))
_NT = (((1,), (1,)), ((), ()))


def _dot(a, b, dn, hi):
    if hi:
        return lax.dot_general(a, b, dn, preferred_element_type=F32, precision=HI)
    return lax.dot_general(a.astype(BF16), b.astype(BF16), dn, preferred_element_type=F32)


def _mm(a, b, hi=True):
    return _dot(a, b, _NN, hi)


def _mm_nt(a, b, hi=True):
    return _dot(a, b, _NT, hi)


def _pieces(x, n):
    out = []
    for _ in range(n - 1):
        p = x.astype(BF16)
        out.append(p)
        x = x - p.astype(F32)
    out.append(x.astype(BF16))
    return out


def _mm3(a, b):
    ah, al = _pieces(a, 2)
    bh, bl = _pieces(b, 2)
    d = lambda x, y: jnp.dot(x, y, preferred_element_type=F32)
    return d(ah, bh) + (d(ah, bl) + d(al, bh))


def _sel_mm(sel, x, n=3):
    acc = None
    for p in _pieces(x, n):
        t = jnp.dot(sel, p, preferred_element_type=F32)
        acc = t if acc is None else acc + t
    return acc


def _mm_sel(x, sel, n=2):
    acc = None
    for p in _pieces(x, n):
        t = jnp.dot(p, sel, preferred_element_type=F32)
        acc = t if acc is None else acc + t
    return acc


def _head_ones(width):
    r = lax.broadcasted_iota(jnp.int32, (width, width), 0) >> 6
    c = lax.broadcasted_iota(jnp.int32, (width, width), 1) >> 6
    return (r == c).astype(BF16)


def _layernorm(x, g, b):
    mu = jnp.mean(x, -1, keepdims=True)
    xc = x - mu
    var = jnp.mean(xc * xc, -1, keepdims=True)
    return xc * lax.rsqrt(var + 1e-5) * g + b


def _head_norm(x, eps):
    ones = _head_ones(x.shape[1])
    mu = _mm_sel(x, ones) * (1.0 / HD)
    xc = x - mu
    var = _mm_sel(xc * xc, ones) * (1.0 / HD)
    return xc * lax.rsqrt(var + eps)


def _mod_kernel(c_ref, w_ref, b_ref, o_ref):
    c = c_ref[...]
    s = c * jax.nn.sigmoid(c)
    o_ref[0] = _mm(s, w_ref[0]) + b_ref[0]


def _mod_all(cvec, w_mod, b_mod):
    tn = 1536
    return pl.pallas_call(
        _mod_kernel,
        grid=(DEPTH, 6 * D // tn),
        in_specs=[pl.BlockSpec((8, D), lambda l, n: (0, 0)),
                  pl.BlockSpec((1, D, tn), lambda l, n: (l, 0, n)),
                  pl.BlockSpec((1, 1, tn), lambda l, n: (l, 0, n))],
        out_specs=pl.BlockSpec((1, 8, tn), lambda l, n: (l, 0, n)),
        out_shape=jax.ShapeDtypeStruct((DEPTH, 8, 6 * D), F32),
        compiler_params=_cparams(("arbitrary", "arbitrary")),
        name="mod",
    )(cvec, w_mod, b_mod.reshape(DEPTH, 1, 6 * D))


def _mod_row_map(rows_per_cond, tm, base):
    return lambda i: (base + (i * tm) // rows_per_cond, 0, 0)


def _inproj_kernel(x_ref, mod_ref, w_ref, o_ref):
    m = mod_ref[0]
    h = x_ref[...] * (1.0 + m[:, D:2 * D]) + m[:, 0:D]
    o_ref[...] = jnp.dot(h.astype(BF16), w_ref[...], preferred_element_type=F32)


def _inproj(x, mod8, w_in_p, rows_per_cond, base):
    n = x.shape[0]
    tm, tn = min(n, 1024), 1152
    rmap = _mod_row_map(rows_per_cond, tm, base)
    return pl.pallas_call(
        _inproj_kernel,
        grid=(PC // tn, n // tm),
        in_specs=[pl.BlockSpec((tm, D), lambda c, i: (i, 0)),
                  pl.BlockSpec((1, 1, 6 * D), lambda c, i: rmap(i)),
                  pl.BlockSpec((D, tn), lambda c, i: (0, c))],
        out_specs=pl.BlockSpec((tm, tn), lambda c, i: (i, c)),
        out_shape=jax.ShapeDtypeStruct((n, PC), F32),
        compiler_params=_cparams(("arbitrary", "arbitrary")),
        name="inproj",
    )(x, mod8, w_in_p)


def _outmlp_kernel(yh, hf, hb, og, mlg, yf, yb, bon, rg, lng, lnb, ya, x_ref, mod_ref,
                   wo, g1, b1, w1, w2, g2, b2, o_ref):
    m = mod_ref[0]
    y_ml = _head_norm(hf[...] + hb[...], 1e-6) * mlg[...] * jax.nn.sigmoid(og[...])
    y_rw = (_head_norm(yf[...] + yb[...], RW_GN_EPS) * lng[...] + lnb[...] + bon[...]) * rg[...]
    y = jnp.concatenate([yh[...], y_ml, y_rw, ya[...]], axis=-1).astype(BF16)
    mix = jnp.dot(y, wo[...], preferred_element_type=F32)
    x1 = _layernorm(ALPHA * x_ref[...] + m[:, 2 * D:3 * D] * mix, g1[...], b1[...])
    h = (x1 * (1.0 + m[:, 4 * D:5 * D]) + m[:, 3 * D:4 * D]).astype(BF16)
    acc = jnp.zeros(x1.shape, F32)
    for c in range(D_FF // D):
        a = jnp.dot(h, w1[:, c * D:(c + 1) * D], preferred_element_type=F32)
        a = jnp.square(jnp.maximum(a, 0.0)).astype(BF16)
        acc = acc + jnp.dot(a, w2[c * D:(c + 1) * D, :], preferred_element_type=F32)
    o_ref[...] = _layernorm(ALPHA * x1 + m[:, 5 * D:6 * D] * acc, g2[...], b2[...])


def _outmlp(ys, x, mod8, wo, g1, b1, w1, w2, g2, b2, rows_per_cond, base):
    n = x.shape[0]
    tm = 512
    rmap = _mod_row_map(rows_per_cond, tm, base)
    const = lambda shape: pl.BlockSpec(shape, lambda i: (0, 0), pipeline_mode=pl.Buffered(1))
    row = lambda wd: pl.BlockSpec((tm, wd), lambda i: (i, 0))
    vec = pl.BlockSpec((1, W), lambda i: (0, 0))
    return pl.pallas_call(
        _outmlp_kernel,
        grid=(n // tm,),
        in_specs=[row(W), row(W), row(W), pl.BlockSpec((tm, W), lambda i: (i, C_ML // W + 3)), vec,
                  row(W), row(W), row(W), row(W), vec, vec, row(W), row(D),
                  pl.BlockSpec((1, 1, 6 * D), lambda i: rmap(i)),
                  const((D, D)), const((1, D)), const((1, D)),
                  const((D, D_FF)), const((D_FF, D)), const((1, D)), const((1, D))],
        out_specs=row(D),
        out_shape=jax.ShapeDtypeStruct((n, D), F32),
        compiler_params=_cparams(("arbitrary",)),
        name="outmlp",
    )(*ys, x, mod8, wo, g1, b1, w1, w2, g2, b2)


def _hyfilt_kernel(z_ref, w1, b1, fr, w2, b2, w3, dec_ref, h_ref, nrm_ref):
    i = pl.program_id(0)
    f = fr[...]
    hid = jnp.sin(f * (_mm(z_ref[...], w1[...]) + b1[...]))
    hid = jnp.sin(f * (_mm(hid, w2[...]) + b2[...]))
    h = _mm(hid, w3[...])
    dec = dec_ref[...]
    h = h * jnp.concatenate([dec, dec, dec, dec], axis=1)
    rid = lax.broadcasted_iota(jnp.int32, h.shape, 0)
    cid = lax.broadcasted_iota(jnp.int32, h.shape, 1)
    anti = ((cid >> 8) & 1) == 1
    h = jnp.where(anti & (rid == 0) & (i == 0), 0.0, h)
    h_ref[...] = h.astype(BF16)
    part = jnp.broadcast_to(jnp.sum(jnp.abs(h), axis=0, keepdims=True), (8, 4 * W))

    @pl.when(i == 0)
    def _():
        nrm_ref[...] = part

    @pl.when(i > 0)
    def _():
        nrm_ref[...] = nrm_ref[...] + part


def _hyfilt(L, z, w1p, b1, fr, w2, b2, w3, dec):
    tl = min(L, 512)
    const = lambda shape: pl.BlockSpec(shape, lambda i: (0, 0))
    return pl.pallas_call(
        _hyfilt_kernel,
        grid=(L // tl,),
        in_specs=[pl.BlockSpec((tl, 128), lambda i: (i, 0)),
                  const((128, HY_FFN)), const((1, HY_FFN)), const((1, HY_FFN)),
                  const((HY_FFN, HY_FFN)), const((1, HY_FFN)), const((HY_FFN, 4 * W)),
                  pl.BlockSpec((tl, W), lambda i: (i, 0))],
        out_specs=[pl.BlockSpec((tl, 4 * W), lambda i: (i, 0)), const((8, 4 * W))],
        out_shape=[jax.ShapeDtypeStruct((L, 4 * W), BF16), jax.ShapeDtypeStruct((8, 4 * W), F32)],
        compiler_params=_cparams(("arbitrary",)),
        name="hyfilt",
    )(z, w1p, b1, fr, w2, b2, w3, dec)


def _hyspec_kernel(fc, fs, h_ref, nrm_ref, re_ref, im_ref, *, L):
    k = pl.program_id(0)
    h = h_ref[...]
    re = jnp.dot(fc[...], h, preferred_element_type=F32)
    im = jnp.dot(fs[...], h, preferred_element_type=F32)
    tk = re.shape[0]
    rid = lax.broadcasted_iota(jnp.int32, (tk, W), 0) + k * tk
    packed = rid == 0
    scale = jnp.where(packed, 1.0, 2.0) * (1.0 / (2 * L))
    nrm = nrm_ref[0:1, :]
    res, ims = [], []
    for f in range(2):
        c0, c1 = 2 * f * W, (2 * f + 1) * W
        inv = scale / (nrm[:, c0:c0 + W] + nrm[:, c1:c1 + W])
        res.append((re[:, c0:c0 + W] + re[:, c1:c1 + W]) * inv)
        ims.append(jnp.where(packed, im[:, c0:c0 + W] + im[:, c1:c1 + W],
                             im[:, c0:c0 + W] - im[:, c1:c1 + W]) * inv)
    re_ref[...] = jnp.concatenate(res, axis=1)
    im_ref[...] = jnp.concatenate(ims, axis=1)


def _hyspec(L, fc, fs, h, nrm):
    tk = min(L, 512)
    return pl.pallas_call(
        functools.partial(_hyspec_kernel, L=L),
        grid=(L // tk,),
        in_specs=[pl.BlockSpec((tk, L), lambda k: (k, 0)),
                  pl.BlockSpec((tk, L), lambda k: (k, 0)),
                  pl.BlockSpec((L, 4 * W), lambda k: (0, 0), pipeline_mode=pl.Buffered(1)),
                  pl.BlockSpec((8, 4 * W), lambda k: (0, 0))],
        out_specs=[pl.BlockSpec((tk, 2 * W), lambda k: (k, 0))] * 2,
        out_shape=[jax.ShapeDtypeStruct((L, 2 * W), F32)] * 2,
        compiler_params=_cparams(("arbitrary",)),
        name="hyspec",
    )(fc, fs, h, nrm)


def _seqs_per_step(B, L):
    return max(1, min(B, 2048 // L))


def _hypre_kernel(x_ref, w_ref, o_ref, *, L):
    x = x_ref[...]
    R = x.shape[0]
    w = w_ref[0]
    pos = lax.broadcasted_iota(jnp.int32, x.shape, 0) & (L - 1)
    xp = jnp.where(pos == 0, 0.0, pltpu.roll(x, 1, axis=0))
    xn = jnp.where(pos == L - 1, 0.0, pltpu.roll(x, R - 1, axis=0))
    o_ref[...] = xp * w[0:1, :] + x * w[1:2, :] + xn * w[2:3, :]


def _hypre(p, B, L, conv_w3):
    assert L & (L - 1) == 0
    nb = _seqs_per_step(B, L)
    return pl.pallas_call(
        functools.partial(_hypre_kernel, L=L),
        grid=(B // nb, 3),
        in_specs=[pl.BlockSpec((nb * L, W), lambda b, c: (b, c)),
                  pl.BlockSpec((1, 3, W), lambda b, c: (c, 0, 0))],
        out_specs=pl.BlockSpec((nb * L, W), lambda b, c: (b, c)),
        out_shape=jax.ShapeDtypeStruct((B * L, 3 * W), F32),
        compiler_params=_cparams(("arbitrary", "arbitrary")),
        name="hypre",
    )(p, conv_w3)


def _hyfwd_kernel(u_ref, fc, fs, fre_ref, fim_ref, yre_ref, yim_ref, ub, *, L, nb):
    k = pl.program_id(1)

    @pl.when(k == 0)
    def _():
        ub[...] = u_ref[...].astype(BF16)

    fre, fim = fre_ref[...], fim_ref[...]
    tk = fre.shape[0]
    packed = (lax.broadcasted_iota(jnp.int32, fre.shape, 0) + k * tk) == 0
    for s in range(nb):
        u = ub[s * L:(s + 1) * L, :]
        re = jnp.dot(fc[...], u, preferred_element_type=F32)
        im = jnp.dot(fs[...], u, preferred_element_type=F32)
        yre_ref[s * tk:(s + 1) * tk, :] = jnp.where(packed, re * fre, re * fre - im * fim)
        yim_ref[s * tk:(s + 1) * tk, :] = jnp.where(packed, im * fim, re * fim + im * fre)


def _hyfwd(u, ucol, B, L, fc, fs, fre, fim, f):
    tk = min(L, 512)
    nb = _seqs_per_step(B, L) if tk == L else 1
    nk = L // tk
    return pl.pallas_call(
        functools.partial(_hyfwd_kernel, L=L, nb=nb),
        grid=(B // nb, nk),
        in_specs=[pl.BlockSpec((nb * L, W), lambda b, k: (b, ucol)),
                  pl.BlockSpec((tk, L), lambda b, k: (k, 0)),
                  pl.BlockSpec((tk, L), lambda b, k: (k, 0)),
                  pl.BlockSpec((tk, W), lambda b, k: (k, f)),
                  pl.BlockSpec((tk, W), lambda b, k: (k, f))],
        out_specs=[pl.BlockSpec((nb * tk, W), lambda b, k: (b * nk + k, 0))] * 2,
        out_shape=[jax.ShapeDtypeStruct((B * L, W), F32)] * 2,
        scratch_shapes=[pltpu.VMEM((nb * L, W), BF16)],
        compiler_params=_cparams(("arbitrary", "arbitrary")),
        name="hyfwd",
    )(u, fc, fs, fre, fim)


def _hyinv_kernel(yre_ref, yim_ref, fc, fst, u_ref, g_ref, bias_ref, o_ref, yb, *, L, nb):
    t = pl.program_id(1)

    @pl.when(t == 0)
    def _():
        yb[0] = yre_ref[...].astype(BF16)
        yb[1] = yim_ref[...].astype(BF16)

    tt = fc.shape[0]
    for s in range(nb):
        y = jnp.dot(fc[...], yb[0, s * L:(s + 1) * L, :], preferred_element_type=F32) \
            + jnp.dot(fst[...], yb[1, s * L:(s + 1) * L, :], preferred_element_type=F32)
        rows = slice(s * tt, (s + 1) * tt)
        o_ref[rows, :] = g_ref[rows, :] * (y + u_ref[rows, :] * bias_ref[0])


def _hyinv(yre, yim, B, L, fc, fst, u, ucol, gate, gcol, bias, f):
    tt = min(L, 512)
    nt = L // tt
    nb = _seqs_per_step(B, L) if tt == L else 1
    return pl.pallas_call(
        functools.partial(_hyinv_kernel, L=L, nb=nb),
        grid=(B // nb, nt),
        in_specs=[pl.BlockSpec((nb * L, W), lambda b, t: (b, 0)),
                  pl.BlockSpec((nb * L, W), lambda b, t: (b, 0)),
                  pl.BlockSpec((tt, L), lambda b, t: (t, 0)),
                  pl.BlockSpec((tt, L), lambda b, t: (t, 0)),
                  pl.BlockSpec((nb * tt, W), lambda b, t: (b * nt + t, ucol)),
                  pl.BlockSpec((nb * tt, W), lambda b, t: (b * nt + t, gcol)),
                  pl.BlockSpec((1, 1, W), lambda b, t: (f, 0, 0))],
        out_specs=pl.BlockSpec((nb * tt, W), lambda b, t: (b * nt + t, 0)),
        out_shape=jax.ShapeDtypeStruct((B * L, W), F32),
        scratch_shapes=[pltpu.VMEM((2, nb * L, W), BF16)],
        compiler_params=_cparams(("arbitrary", "arbitrary")),
        name="hyinv",
    )(yre, yim, fc, fst, u, gate, bias)


def _dft_consts(L):
    s = int(round(math.sqrt(L)))
    assert s * s == L
    k = jnp.arange(L, dtype=jnp.int32)[:, None]
    n1 = jnp.arange(s, dtype=jnp.int32)[None, :]
    a1 = ((k * (n1 * s)) % (2 * L)).astype(F32) * (math.pi / L)
    a2 = ((k * n1) % (2 * L)).astype(F32) * (math.pi / L)
    c1, s1, c2, s2 = jnp.cos(a1), jnp.sin(a1), jnp.cos(a2), jnp.sin(a2)
    fc = (c1[:, :, None] * c2[:, None, :] - s1[:, :, None] * s2[:, None, :]).reshape(L, L)
    fs = -(s1[:, :, None] * c2[:, None, :] + c1[:, :, None] * s2[:, None, :]).reshape(L, L)
    nyq = (1.0 - 2.0 * (jnp.arange(L) % 2)).astype(F32)[None, :]
    fs = jnp.where(k == 0, nyq, fs)
    return fc.astype(BF16), fs.astype(BF16), fs.T.astype(BF16)


def _hyena_consts(L):
    t = jnp.linspace(0.0, 1.0, L, dtype=F32)[:, None]
    ang = (2.0 * math.pi / L) * jnp.arange(L, dtype=F32)[:, None]
    bands = jnp.linspace(1e-4, HY_BANDS - 1, HY_BANDS, dtype=F32)[None, :]
    z = jnp.concatenate([t, jnp.cos(bands * ang), jnp.sin(bands * ang)], axis=-1)
    z = jnp.pad(z, ((0, 0), (0, 128 - HY_EMB)))
    deltas = jnp.abs(jnp.linspace(HY_MIN_DECAY, HY_MAX_DECAY, W, dtype=F32))
    dec = jnp.exp(-t * deltas)
    return z, dec


def _hyena(p, B, L, hc, lp):
    z, dec, fc, fs, fst = hc
    w1p = jnp.pad(lp['hy_w1'], ((0, 128 - HY_EMB), (0, 0)))
    row = lambda a: a.reshape(1, -1)
    h, nrm = _hyfilt(L, z, w1p, row(lp['hy_b1']), row(lp['hy_freq']), lp['hy_w2'], row(lp['hy_b2']),
                     lp['hy_w3'], dec)
    fre, fim = _hyspec(L, fc, fs, h, nrm)
    pc = _hypre(p, B, L, lp['hy_conv'].reshape(3, 3, W).transpose(1, 0, 2))
    bias = lp['hy_bias'].reshape(2, 1, W)
    yre, yim = _hyfwd(pc, 0, B, L, fc, fs, fre, fim, 0)
    zmid = _hyinv(yre, yim, B, L, fc, fst, pc, 0, pc, 1, bias, 0)
    yre, yim = _hyfwd(zmid, 0, B, L, fc, fs, fre, fim, 1)
    return _hyinv(yre, yim, B, L, fc, fst, zmid, 0, pc, 2, bias, 1)


def _mlstm_kernel(qf, kf, vf, gf, qb, kb, vb, gb, gbias, c0, n0, m0,
                  hf_o, hb_o, c_o, n_o, m_o, c_s, n_s, m_s, *, nc, nb):
    j = pl.program_id(1)
    T = ML_CHUNK

    @pl.when(j == 0)
    def _():
        c_s[...] = c0[...]
        n_s[...] = n0[...]
        m_s[...] = m0[...]

    lane = lax.broadcasted_iota(jnp.int32, (T, 128), 1)
    is_f = ((lane & 4) != 0) & (lane < 16)
    row = lax.broadcasted_iota(jnp.int32, (T, T), 0)
    col = lax.broadcasted_iota(jnp.int32, (T, T), 1)
    masks = (row >= col, row <= col)
    refs = ((qf, kf, vf, gf), (qb, kb, vb, gb))
    grp = [(d, i) for d in range(2) for i in range(nb)]
    Gs = []
    for d, i in grp:
        g = refs[d][3][i] + gbias[...]
        Gs.append(jnp.where(is_f, jnp.minimum(g, 0.0) - jnp.log1p(jnp.exp(-jnp.abs(g))), g))
    Bcs = [_sel_mm(masks[d].astype(BF16), G) for (d, i), G in zip(grp, Gs)]
    sr = lax.broadcasted_iota(jnp.int32, (128, NH * 128), 0)
    sc = lax.broadcasted_iota(jnp.int32, (128, NH * 128), 1) >> 7
    units = []
    for gi, (d, i) in enumerate(grp):
        G, Bc = Gs[gi], Bcs[gi]
        GT, BcT = G.T, Bc.T
        Bcol = _mm_sel(Bc, (sr == sc + (d * 8 + 4)).astype(BF16), 3)
        Icol = _mm_sel(G, (sr == sc + d * 8).astype(BF16), 3)
        q32 = refs[d][0][i] * (HD ** -0.5)
        q = q32.astype(BF16)
        k = refs[d][1][i]
        v = refs[d][2][i].astype(BF16)
        for h in range(NH):
            cf, ci = d * 8 + 4 + h, d * 8 + h
            sl = slice(h * HD, (h + 1) * HD)
            b_col = Bcol[:, h * 128:(h + 1) * 128]
            units.append(dict(
                d=d, i=i, u=d * NH + h, q=q[:, sl], q32=q32[:, sl], k=k[:, sl], v=v[:, sl],
                b_col=b_col, b_row=BcT[cf:cf + 1, :], i_col=Icol[:, h * 128:(h + 1) * 128], i_row=GT[ci:ci + 1, :],
                b_tot=b_col[T - 1:T, :] if d == 0 else b_col[0:1, :]))
    for un in units:
        un['m_prev'] = m_s[un['i'], un['u']:un['u'] + 1, :]
        un['C'] = c_s[un['i'], un['u']]
        un['n'] = n_s[un['i'], un['u']:un['u'] + 1, :]
    for un in units:
        un['qk'] = _mm_nt(un['q'], un['k'], False)
    for un in units:
        un['qC'] = _mm(un['q'], un['C'], False)
    for un in units:
        logd = jnp.where(masks[un['d']], un['b_col'] - un['b_row'] + un['i_row'], -jnp.inf)
        inter = un['b_col'] + un['m_prev']
        m_t = jnp.maximum(inter, jnp.max(logd, axis=-1, keepdims=True))
        un['s'] = un['qk'] * jnp.exp(logd - m_t)
        un['w_in'] = jnp.exp(inter - m_t)
        un['floor'] = jnp.exp(-m_t)
        logw = un['b_tot'] - un['b_col'] + un['i_col']
        m_new = jnp.maximum(un['b_tot'] + un['m_prev'], jnp.max(logw, axis=0, keepdims=True))
        un['kw'] = un['k'] * jnp.exp(logw - m_new)[:, 0:HD]
        un['decay'] = jnp.exp(un['b_tot'] + un['m_prev'] - m_new)[:, 0:HD]
        un['m_new'] = m_new
    for un in units:
        un['sv'] = _mm(un['s'], un['v'], False)
    for un in units:
        un['kv'] = _mm(un['kw'].T, un['v'], False)
    hs = {}
    for un in units:
        i, u = un['i'], un['u']
        num = un['sv'] + un['w_in'][:, 0:HD] * un['qC']
        den = jnp.sum(un['s'], axis=-1, keepdims=True) \
            + un['w_in'] * jnp.sum(un['q32'] * un['n'], axis=-1, keepdims=True)
        hs.setdefault((un['d'], i), []).append(num / jnp.maximum(jnp.abs(den), un['floor'])[:, 0:HD])
        c_s[i, u] = un['decay'] * un['C'] + un['kv']
        n_s[i, u:u + 1, :] = un['decay'] * un['n'] + jnp.sum(un['kw'], axis=0, keepdims=True)
        m_s[i, u:u + 1, :] = un['m_new']
    for (d, i), parts in hs.items():
        (hf_o, hb_o)[d][i] = jnp.concatenate(parts, axis=1)

    @pl.when(j == nc - 1)
    def _():
        c_o[...] = c_s[...]
        n_o[...] = n_s[...]
        m_o[...] = m_s[...]


def _mlstm_scan(p, B, L, gbias, c0, n0, m0):
    T = ML_CHUNK
    nc = L // T
    nb = 2
    p3 = p.reshape(B, L, PC)
    fwd = lambda cb: (lambda b, j: (b, j, cb))
    bwd = lambda cb: (lambda b, j: (b, nc - 1 - j, cb))
    blk = lambda f, cb, wd=W: pl.BlockSpec((nb, T, wd), f(cb))
    cq, ck, cv, cg = C_ML // W, C_ML // W + 1, C_ML // W + 2, C_MLG // 128
    st = lambda shape: pl.BlockSpec((nb,) + shape, lambda b, j: (b,) + (0,) * len(shape))
    hf, hb, C, n, m = pl.pallas_call(
        functools.partial(_mlstm_kernel, nc=nc, nb=nb),
        grid=(B // nb, nc),
        in_specs=[blk(fwd, cq), blk(fwd, ck), blk(fwd, cv), blk(fwd, cg, 128),
                  blk(bwd, cq), blk(bwd, ck), blk(bwd, cv), blk(bwd, cg, 128),
                  pl.BlockSpec((1, 128), lambda b, j: (0, 0)),
                  st((8, HD, HD)), st((8, HD)), st((8, 128))],
        out_specs=[pl.BlockSpec((nb, T, W), lambda b, j: (b, j, 0)),
                   pl.BlockSpec((nb, T, W), lambda b, j: (b, nc - 1 - j, 0)),
                   st((8, HD, HD)), st((8, HD)), st((8, 128))],
        out_shape=[jax.ShapeDtypeStruct((B, L, W), F32), jax.ShapeDtypeStruct((B, L, W), F32),
                   jax.ShapeDtypeStruct((B, 8, HD, HD), F32), jax.ShapeDtypeStruct((B, 8, HD), F32),
                   jax.ShapeDtypeStruct((B, 8, 128), F32)],
        scratch_shapes=[pltpu.VMEM((nb, 8, HD, HD), F32), pltpu.VMEM((nb, 8, HD), F32),
                        pltpu.VMEM((nb, 8, 128), F32)],
        compiler_params=_cparams(("arbitrary", "arbitrary")),
        name="mlstm",
    )(p3, p3, p3, p3, p3, p3, p3, p3, gbias, c0, n0, m0)
    return hf.reshape(B * L, W), hb.reshape(B * L, W), C, n, m


def _shifted(x, prev8, next8, first, last):
    T = x.shape[0]
    rid = lax.broadcasted_iota(jnp.int32, x.shape, 0)
    pr = jnp.where(first, 0.0, prev8[7:8, :])
    nx = jnp.where(last, 0.0, next8[0:1, :])
    xp = jnp.where(rid == 0, pr, pltpu.roll(x, 1, axis=0))
    xn = jnp.where(rid == T - 1, nx, pltpu.roll(x, T - 1, axis=0))
    return xp, xn


def _rwprep_kernel(*refs, L, T):
    xs, prevs, nexts = refs[0:4], refs[4:8], refs[8:12]
    mu, w0, w2, a0, a2, g2, kkw, kaw, rkw = refs[12:21]
    r_o, v_o, kk_o, g_o, bon_o, lw0_o, lw1_o, kt0_o, kt1_o, ba0_o, ba1_o = refs[21:]
    i = pl.program_id(0)
    first = (i * T) % L == 0
    last = ((i + 1) * T) % L == 0
    pf = []
    for c in range(4):
        x = xs[c][...]
        xp, xn = _shifted(x, prevs[c][...], nexts[c][...], first, last)
        pf.append(x + mu[:, c * W:(c + 1) * W] * (0.5 * (xp + xn) - x))
    r, k, v, misc = pf
    lw, la, lg = misc[:, 0:64], misc[:, 64:128], misc[:, 128:256]
    ones = _head_ones(W)
    g_o[...] = _mm3(jax.nn.sigmoid(lg), g2[...])
    kq = k * kkw[...]
    kk = kq / jnp.maximum(jnp.sqrt(_mm_sel(kq * kq, ones)), 1e-12)
    tw = jnp.tanh(lw)
    r_o[...] = r
    v_o[...] = v
    kk_o[...] = kk
    bon = jnp.zeros_like(r)
    for d, (lw_o, kt_o, ba_o) in enumerate(((lw0_o, kt0_o, ba0_o), (lw1_o, kt1_o, ba1_o))):
        lw_o[...] = -RW_DECAY_SCALE * jax.nn.sigmoid(w0[d:d + 1, :] + _mm3(tw, w2[d]))
        a = jax.nn.sigmoid(a0[d:d + 1, :] + _mm3(la, a2[d]))
        kt = k * (1.0 + (a - 1.0) * kaw[...])
        kt_o[...] = kt
        ba_o[...] = kk * a
        bon = bon + _mm_sel(r * kt * rkw[...], ones) * v
    bon_o[...] = bon


def _rwprep(p, B, L, lp):
    n = B * L
    T = min(L, 512)
    nb8 = n // 8
    c0 = C_RW // W
    cur = lambda c: pl.BlockSpec((T, W), lambda i: (i, c0 + c))
    prv = lambda c: pl.BlockSpec((8, W), lambda i: (jnp.maximum(i * (T // 8) - 1, 0), c0 + c))
    nxt = lambda c: pl.BlockSpec((8, W), lambda i: (jnp.minimum((i + 1) * (T // 8), nb8 - 1), c0 + c))
    const = lambda shape: pl.BlockSpec(shape, lambda i: (0,) * len(shape))
    row = lambda a: a.reshape(1, -1)
    out = pl.BlockSpec((T, W), lambda i: (i, 0))
    return pl.pallas_call(
        functools.partial(_rwprep_kernel, L=L, T=T),
        grid=(n // T,),
        in_specs=[cur(c) for c in range(4)] + [prv(c) for c in range(4)] + [nxt(c) for c in range(4)]
        + [const((1, 4 * W)), const((2, W)), const((2, 64, W)), const((2, W)), const((2, 64, W)),
           const((128, W)), const((1, W)), const((1, W)), const((1, W))],
        out_specs=[out] * 11,
        out_shape=[jax.ShapeDtypeStruct((n, W), F32)] * 11,
        compiler_params=_cparams(("arbitrary",)),
        name="rwprep",
    )(*([p] * 12), row(lp['rw_mu']), lp['rw_w0'], lp['rw_w2'], lp['rw_a0'], lp['rw_a2'], lp['rw_g2'],
      row(lp['rw_kk']), row(lp['rw_ka']), row(lp['rw_rk']))


def _rwscan_kernel(rf, vf, kkf, lwf, ktf, baf, rb, vb, kkb, lwb, ktb, bab, s0,
                   yf_o, yb_o, s_o, s_s, *, nc, nb):
    j = pl.program_id(1)
    T = RW_CHUNK

    @pl.when(j == 0)
    def _():
        s_s[...] = s0[...]

    row = lax.broadcasted_iota(jnp.int32, (T, T), 0)
    col = lax.broadcasted_iota(jnp.int32, (T, T), 1)
    incl = (row >= col, row <= col)
    strict = (row > col, row < col)
    refs = ((rf, vf, kkf, lwf, ktf, baf), (rb, vb, kkb, lwb, ktb, bab))
    grp = [(d, i) for d in range(2) for i in range(nb)]
    lws = [refs[d][3][i] for d, i in grp]
    lcs = [_sel_mm(incl[d].astype(BF16), lw) for (d, i), lw in zip(grp, lws)]
    units = []
    for (d, i), lw, lc in zip(grp, lws, lcs):
        r_ref, v_ref, kk_ref, _, kt_ref, ba_ref = refs[d]
        gam = jnp.exp(lc)
        ginv = jnp.exp(-lc)
        at = -kk_ref[i] * jnp.exp(lc - lw)
        ar = jnp.concatenate([at, r_ref[i] * gam], axis=0).astype(BF16)
        bk = jnp.concatenate([ba_ref[i] * ginv, kt_ref[i] * ginv], axis=0).astype(BF16)
        v = v_ref[i]
        vb16 = v.astype(BF16)
        gtot = gam[T - 1:T, :] if d == 0 else gam[0:1, :]
        for h in range(NH):
            sl = slice(h * HD, (h + 1) * HD)
            units.append(dict(d=d, i=i, u=d * NH + h, ar=ar[:, sl], bk=bk[:, sl], v=v[:, sl], vb=vb16[:, sl],
                              gtot=gtot[:, sl]))
    for un in units:
        un['S'] = s_s[un['i'], un['u']]
    for un in units:
        un['P'] = lax.dot_general(un['ar'], un['bk'], _NT, preferred_element_type=F32)
    for un in units:
        un['AS'] = _mm_nt(un['ar'], un['S'], False)
    for un in units:
        P, d = un['P'], un['d']
        un['A_ab'] = jnp.where(strict[d], P[0:T, 0:T], 0.0)
        un['A_rb'] = jnp.where(incl[d], P[T:2 * T, 0:T], 0.0).astype(BF16)
        un['A_k'] = jnp.concatenate([jnp.where(strict[d], P[0:T, T:2 * T], 0.0),
                                     jnp.where(incl[d], P[T:2 * T, T:2 * T], 0.0)], axis=0).astype(BF16)
    for un in units:
        un['AV'] = jnp.dot(un['A_k'], un['vb'], preferred_element_type=F32)
    pair = ((row >> 1) == (col >> 1)) & (row != col)
    eye = (row == col).astype(F32)
    for un in units:
        un['X'] = eye + jnp.where(pair, un['A_ab'], 0.0)
    bs = 2
    while bs < T:
        sh = bs.bit_length() - 1
        pair = ((row >> (sh + 1)) == (col >> (sh + 1))) & ((row >> sh) != (col >> sh))
        for un in units:
            un['Xb'] = un['X'].astype(BF16)
            un['XN'] = jnp.dot(un['Xb'], jnp.where(pair, un['A_ab'], 0.0).astype(BF16), preferred_element_type=F32)
        for un in units:
            un['X'] = un['X'] + jnp.dot(un['XN'].astype(BF16), un['Xb'], preferred_element_type=F32)
        bs *= 2
    for un in units:
        un['U'] = _mm(un['X'], un['AS'][0:T] + un['AV'][0:T], False)
    for un in units:
        un['Y'] = un['AS'][T:2 * T] + un['AV'][T:2 * T] + jnp.dot(un['A_rb'], un['U'].astype(BF16),
                                                                  preferred_element_type=F32)
    for un in units:
        UV = jnp.concatenate([un['U'], un['v']], axis=0)
        s_s[un['i'], un['u']] = (un['S'] + _mm(UV.T, un['bk'], False)) * un['gtot']
    ys = {}
    for un in units:
        ys.setdefault((un['d'], un['i']), []).append(un['Y'])
    for (d, i), parts in ys.items():
        (yf_o, yb_o)[d][i] = jnp.concatenate(parts, axis=1)

    @pl.when(j == nc - 1)
    def _():
        s_o[...] = s_s[...]


def _rwscan(prep, B, L, s0):
    T = RW_CHUNK
    nc = L // T
    nb = 2
    r, v, kk, lw0, lw1, kt0, kt1, ba0, ba1 = [a.reshape(B, L, W) for a in prep]
    fwd = pl.BlockSpec((nb, T, W), lambda b, j: (b, j, 0))
    bwd = pl.BlockSpec((nb, T, W), lambda b, j: (b, nc - 1 - j, 0))
    st = pl.BlockSpec((nb, 8, HD, HD), lambda b, j: (b, 0, 0, 0))
    yf, yb, S = pl.pallas_call(
        functools.partial(_rwscan_kernel, nc=nc, nb=nb),
        grid=(B // nb, nc),
        in_specs=[fwd] * 6 + [bwd] * 6 + [st],
        out_specs=[fwd, bwd, st],
        out_shape=[jax.ShapeDtypeStruct((B, L, W), F32), jax.ShapeDtypeStruct((B, L, W), F32),
                   jax.ShapeDtypeStruct((B, 8, HD, HD), F32)],
        scratch_shapes=[pltpu.VMEM((nb, 8, HD, HD), F32)],
        compiler_params=_cparams(("arbitrary", "arbitrary")),
        name="rwscan",
    )(r, v, kk, lw0, kt0, ba0, r, v, kk, lw1, kt1, ba1, s0)
    return yf.reshape(B * L, W), yb.reshape(B * L, W), S


def _rms(x, gain):
    ms = _mm_sel(x * x, _head_ones(x.shape[1])) * (1.0 / HD)
    return x * lax.rsqrt(ms + 1e-6) * gain


def _rope(x, cos, sin):
    wd = x.shape[1]
    lane = lax.broadcasted_iota(jnp.int32, x.shape, 1)
    partner = jnp.where((lane & 16) == 0, pltpu.roll(x, wd - 16, axis=1), pltpu.roll(x, 16, axis=1))
    return x * cos + partner * sin


def _attn_kernel(*refs, rope, cached, tq, kchunk):
    if rope:
        q_ref, k_ref, v_ref, qn, kn, cosq, sinq, cosk, sink = refs[:9]
        rest = refs[9:]
    else:
        q_ref, k_ref, v_ref, qn, kn = refs[:5]
        rest = refs[5:]
    if cached:
        kc_ref, vc_ref = rest[:2]
        rest = rest[2:]
    o_ref, kn_o, k_s = rest
    t = pl.program_id(1)

    @pl.when(t == 0)
    def _():
        k = _rms(k_ref[...], kn[...])
        if rope:
            k = _rope(k, cosk[...], sink[...])
        kn_o[...] = k
        k_s[...] = k.astype(BF16)

    q = _rms(q_ref[...], qn[...])
    if rope:
        q = _rope(q, cosq[...], sinq[...])
    q = (q * (HD ** -0.5 * math.log2(math.e))).astype(BF16)
    Lk = k_s.shape[0]
    for jkv in range(2):
        sl = slice(jkv * HD, (jkv + 1) * HD)
        q2 = jnp.concatenate([q[:, (2 * jkv) * HD:(2 * jkv + 1) * HD],
                              q[:, (2 * jkv + 1) * HD:(2 * jkv + 2) * HD]], axis=0)
        m = jnp.full((2 * tq, 1), -jnp.inf, F32)
        acc = jnp.zeros((2 * tq, 2 * HD), F32)

        def with_ones(v):
            return jnp.concatenate([v.astype(BF16), jnp.ones(v.shape, BF16)], axis=1)

        chunks = [(k_s[c:c + kchunk, sl], with_ones(v_ref[c:c + kchunk, sl])) for c in range(0, Lk, kchunk)]
        if cached:
            chunks.append((kc_ref[0][:, sl].astype(BF16), with_ones(vc_ref[0][:, sl])))
        for kc, vc in chunks:
            s = _mm_nt(q2, kc, hi=False)
            m_new = jnp.maximum(m, jnp.max(s, axis=-1, keepdims=True))
            pr = jnp.exp2(s - m_new)
            acc = jnp.exp2(m - m_new) * acc + jnp.dot(pr.astype(BF16), vc, preferred_element_type=F32)
            m = m_new
        o = acc[:, 0:HD] / acc[:, HD:2 * HD]
        o_ref[:, (2 * jkv) * HD:(2 * jkv + 1) * HD] = o[0:tq]
        o_ref[:, (2 * jkv + 1) * HD:(2 * jkv + 2) * HD] = o[tq:2 * tq]


def _attn(p, B, L, qn, kn, rope_tabs=None, cache=None):
    tq = min(L, 256)
    nt = L // tq
    kchunk = min(L, 1024)
    rope, cached = rope_tabs is not None, cache is not None
    cq, ck, cv = C_AT // W, (C_AT + W) // 128, (C_AT + W) // 128 + 1
    const = lambda shape: pl.BlockSpec(shape, lambda b, t: (0, 0))
    in_specs = [pl.BlockSpec((tq, W), lambda b, t: (b * nt + t, cq)),
                pl.BlockSpec((L, 128), lambda b, t: (b, ck)),
                pl.BlockSpec((L, 128), lambda b, t: (b, cv)),
                const((1, W)), const((1, 128))]
    args = [p, p, p, qn, kn]
    if rope:
        cos, sin = rope_tabs
        in_specs += [pl.BlockSpec((tq, W), lambda b, t: (t, 0)), pl.BlockSpec((tq, W), lambda b, t: (t, 0)),
                     const((L, 128)), const((L, 128))]
        args += [cos, sin, cos, sin]
    if cached:
        kc, vc = cache
        P = kc.shape[1]
        in_specs += [pl.BlockSpec((1, P, 128), lambda b, t: (b, 0, 0))] * 2
        args += [kc, vc]
    return pl.pallas_call(
        functools.partial(_attn_kernel, rope=rope, cached=cached, tq=tq, kchunk=kchunk),
        grid=(B, nt),
        in_specs=in_specs,
        out_specs=[pl.BlockSpec((tq, W), lambda b, t: (b * nt + t, 0)),
                   pl.BlockSpec((L, 128), lambda b, t: (b, 0))],
        out_shape=[jax.ShapeDtypeStruct((B * L, W), F32), jax.ShapeDtypeStruct((B * L, 128), F32)],
        scratch_shapes=[pltpu.VMEM((L, 128), BF16)],
        compiler_params=_cparams(("arbitrary", "arbitrary")),
        name="attn",
    )(*args)


def _rope_tables(L):
    t = jnp.arange(L)
    pos_r = (t // GRID_W).astype(F32)[:, None]
    pos_c = (t % GRID_W).astype(F32)[:, None]
    nf = HD // 4
    inv = ROPE_THETA ** (-jnp.arange(nf, dtype=F32) / nf)[None, :]
    ar, ac = pos_r * inv, pos_c * inv
    cos = jnp.concatenate([jnp.cos(ar), jnp.cos(ar), jnp.cos(ac), jnp.cos(ac)], axis=1)
    sin = jnp.concatenate([-jnp.sin(ar), jnp.sin(ar), -jnp.sin(ac), jnp.sin(ac)], axis=1)
    return jnp.tile(cos, (1, NH)), jnp.tile(sin, (1, NH))


def _layer(x, B, L, mod8, base, rows_per_cond, lp, consts, ctx):
    p = _inproj(x, mod8, lp['w_in_p'], rows_per_cond, base)
    y_hy = _hyena(p, B, L, consts['hy'], lp)

    if ctx is None:
        c0 = jnp.zeros((B, 8, HD, HD), F32)
        n0 = jnp.zeros((B, 8, HD), F32)
        m0 = jnp.zeros((B, 8, 128), F32)
        s0 = jnp.zeros((B, 8, HD, HD), F32)
        cache = None
    else:
        kc, vc, c0, n0, m0, s0 = ctx
        P = kc.shape[1]
        cache = (kc.reshape(B, P, 128), vc.reshape(B, P, 128))
        c0 = c0.reshape(B, 8, HD, HD)
        n0 = n0.reshape(B, 8, HD)
        m0 = jnp.broadcast_to(m0.reshape(B, 8, 1), (B, 8, 128))
        s0 = s0.reshape(B, 8, HD, HD)

    gbias = jnp.pad(lp['ml_gate_b'], (0, 128 - 4 * NH)).reshape(1, 128)
    hf, hb, C, n, m = _mlstm_scan(p, B, L, gbias, c0, n0, m0)

    r, v, kk, g, bon, lw0, lw1, kt0, kt1, ba0, ba1 = _rwprep(p, B, L, lp)
    yf, yb, S = _rwscan((r, v, kk, lw0, lw1, kt0, kt1, ba0, ba1), B, L, s0)

    qn = jnp.tile(lp['at_qn'], NH).reshape(1, W)
    kn = jnp.tile(lp['at_kn'], 2).reshape(1, 128)
    y_at, k_new = _attn(p, B, L, qn, kn, consts.get('rope'), cache)
    v_new = p[:, C_AT + W + 128:C_AT + W + 256]

    row = lambda a: a.reshape(1, D)
    vec = lambda a: a.reshape(1, W)
    ys = (y_hy, hf, hb, p, vec(lp['ml_norm_g']), yf, yb, bon, g, vec(lp['rw_ln_g']), vec(lp['rw_ln_b']), y_at)
    x = _outmlp(ys, x, mod8, lp['w_out_b'], row(lp['ln1_g']), row(lp['ln1_b']),
                lp['mlp_w1_b'], lp['mlp_w2_b'], row(lp['ln2_g']), row(lp['ln2_b']), rows_per_cond, base)
    state = (k_new.reshape(B, L, 2, HD), v_new.reshape(B, L, 2, HD), C.reshape(B, 2, NH, HD, HD),
             n.reshape(B, 2, NH, HD), m[:, :, 0].reshape(B, 2, NH), S.reshape(B, 2, NH, HD, HD))
    return x, state


def _permute_w_in(w_in):
    hy = w_in[:, :, 0:768]
    ml = w_in[:, :, 768:1792]
    mlg = w_in[:, :, 1792:1808]
    rw = w_in[:, :, 1808:2832]
    at = w_in[:, :, 2832:3344]
    mlg = jnp.pad(mlg, ((0, 0), (0, 0), (0, 128 - 16)))
    return jnp.concatenate([hy, ml, rw, at, mlg], axis=-1).astype(BF16)


def kernel(x_prompt, x_sample, cache_attn_k, cache_attn_v, state_mlstm_C, state_mlstm_n, state_mlstm_m, state_rwkv_S, c, c_ctx, w_mod, b_mod, w_in, hy_conv, hy_w1, hy_b1, hy_freq, hy_w2, hy_b2, hy_w3, hy_bias, ml_gate_b, ml_norm_g, rw_mu, rw_w0, rw_w2, rw_a0, rw_a2, rw_g2, rw_kk, rw_ka, rw_rk, rw_ln_g, rw_ln_b, at_qn, at_kn, w_out, ln1_g, ln1_b, mlp_w1, mlp_w2, ln2_g, ln2_b):
    Bp, Lp, _ = x_prompt.shape
    Bs, Ls, _ = x_sample.shape
    params = {
        'w_in_p': _permute_w_in(w_in), 'hy_conv': hy_conv, 'hy_w1': hy_w1, 'hy_b1': hy_b1, 'hy_freq': hy_freq,
        'hy_w2': hy_w2, 'hy_b2': hy_b2, 'hy_w3': hy_w3, 'hy_bias': hy_bias,
        'ml_gate_b': ml_gate_b, 'ml_norm_g': ml_norm_g,
        'rw_mu': rw_mu, 'rw_w0': rw_w0, 'rw_w2': rw_w2, 'rw_a0': rw_a0, 'rw_a2': rw_a2, 'rw_g2': rw_g2,
        'rw_kk': rw_kk, 'rw_ka': rw_ka, 'rw_rk': rw_rk, 'rw_ln_g': rw_ln_g, 'rw_ln_b': rw_ln_b,
        'at_qn': at_qn, 'at_kn': at_kn, 'w_out_b': w_out.astype(BF16), 'ln1_g': ln1_g, 'ln1_b': ln1_b,
        'mlp_w1_b': mlp_w1.astype(BF16), 'mlp_w2_b': mlp_w2.astype(BF16), 'ln2_g': ln2_g, 'ln2_b': ln2_b,
    }
    cvec = jnp.concatenate([c_ctx[None, :], c, jnp.zeros((8 - 1 - Bs, D), F32)], axis=0)
    mod = _mod_all(cvec, w_mod, b_mod).reshape(DEPTH, 8, 1, 6 * D)

    consts_p = {'hy': _hyena_consts(Lp) + _dft_consts(Lp)}
    consts_s = {'hy': _hyena_consts(Ls) + _dft_consts(Ls), 'rope': _rope_tables(Ls)}

    x = x_prompt.reshape(Bp * Lp, D)
    states = []
    for l in range(DEPTH):
        lp = {name: arr[l] for name, arr in params.items()}
        x, st = _layer(x, Bp, Lp, mod[l], 0, Bp * Lp, lp, consts_p, None)
        states.append(st)
    y_prompt = x.reshape(Bp, Lp, D)
    outs = [jnp.stack([st[i] for st in states], axis=1) for i in range(6)]

    x = x_sample.reshape(Bs * Ls, D)
    for l in range(DEPTH):
        lp = {name: arr[l] for name, arr in params.items()}
        ctx = (cache_attn_k[:, l], cache_attn_v[:, l], state_mlstm_C[:, l], state_mlstm_n[:, l],
               state_mlstm_m[:, l], state_rwkv_S[:, l])
        x, _ = _layer(x, Bs, Ls, mod[l], 1, Ls, lp, consts_s, ctx)
    y_sample = x.reshape(Bs, Ls, D)
    return (y_prompt, y_sample) + tuple(outs)
```

```python
import functools
import math

import jax
import jax.numpy as jnp
from jax import lax
from jax.experimental import pallas as pl
from jax.experimental.pallas import tpu as pltpu

F32 = jnp.float32
BF16 = jnp.bfloat16
HI = lax.Precision.HIGHEST

D = 1024
DEPTH = 4
W = 256
HD = 64
NH = 4
GRID_W = 64
D_FF = 4 * D
ALPHA = (2.0 * DEPTH) ** 0.25
HY_BANDS = 16
HY_EMB = 2 * HY_BANDS + 1
HY_FFN = 64
HY_MAX_DECAY = math.log(1e-2) / 0.3
HY_MIN_DECAY = math.log(1e-2) / 1.5
ML_CHUNK = 128
RW_CHUNK = 64
RW_DECAY_SCALE = 0.606531
RW_GN_EPS = 64e-5
ROPE_THETA = 10000.0

PC = 3456
C_HY = 0
C_ML = 768
C_RW = 1792
C_RWX = 2560
C_AT = 2816
C_MLG = 3328

VMEM_LIMIT = 56 * 1024 * 1024


def _cparams(sem):
    return pltpu.CompilerParams(dimension_semantics=sem, vmem_limit_bytes=VMEM_LIMIT)


_NN = (((1,), (0,)), ((), ()))
_NT = (((1,), (1,)), ((), ()))


def _dot(a, b, dn, hi):
    if hi:
        return lax.dot_general(a, b, dn, preferred_element_type=F32, precision=HI)
    return lax.dot_general(a.astype(BF16), b.astype(BF16), dn, preferred_element_type=F32)


def _mm(a, b, hi=True):
    return _dot(a, b, _NN, hi)


def _mm_nt(a, b, hi=True):
    return _dot(a, b, _NT, hi)


def _pieces(x, n):
    out = []
    for _ in range(n - 1):
        p = x.astype(BF16)
        out.append(p)
        x = x - p.astype(F32)
    out.append(x.astype(BF16))
    return out


def _mm3(a, b):
    ah, al = _pieces(a, 2)
    bh, bl = _pieces(b, 2)
    d = lambda x, y: jnp.dot(x, y, preferred_element_type=F32)
    return d(ah, bh) + (d(ah, bl) + d(al, bh))


def _sel_mm(sel, x, n=3):
    acc = None
    for p in _pieces(x, n):
        t = jnp.dot(sel, p, preferred_element_type=F32)
        acc = t if acc is None else acc + t
    return acc


def _mm_sel(x, sel, n=2):
    acc = None
    for p in _pieces(x, n):
        t = jnp.dot(p, sel, preferred_element_type=F32)
        acc = t if acc is None else acc + t
    return acc


def _head_ones(width):
    r = lax.broadcasted_iota(jnp.int32, (width, width), 0) >> 6
    c = lax.broadcasted_iota(jnp.int32, (width, width), 1) >> 6
    return (r == c).astype(BF16)


def _layernorm(x, g, b):
    mu = jnp.mean(x, -1, keepdims=True)
    xc = x - mu
    var = jnp.mean(xc * xc, -1, keepdims=True)
    return xc * lax.rsqrt(var + 1e-5) * g + b


def _head_norm(x, eps):
    ones = _head_ones(x.shape[1])
    mu = _mm_sel(x, ones) * (1.0 / HD)
    xc = x - mu
    var = _mm_sel(xc * xc, ones) * (1.0 / HD)
    return xc * lax.rsqrt(var + eps)


def _mod_kernel(c_ref, w_ref, b_ref, o_ref):
    c = c_ref[...]
    s = c * jax.nn.sigmoid(c)
    o_ref[0] = _mm(s, w_ref[0]) + b_ref[0]


def _mod_all(cvec, w_mod, b_mod):
    tn = 1536
    return pl.pallas_call(
        _mod_kernel,
        grid=(DEPTH, 6 * D // tn),
        in_specs=[pl.BlockSpec((8, D), lambda l, n: (0, 0)),
                  pl.BlockSpec((1, D, tn), lambda l, n: (l, 0, n)),
                  pl.BlockSpec((1, 1, tn), lambda l, n: (l, 0, n))],
        out_specs=pl.BlockSpec((1, 8, tn), lambda l, n: (l, 0, n)),
        out_shape=jax.ShapeDtypeStruct((DEPTH, 8, 6 * D), F32),
        compiler_params=_cparams(("arbitrary", "arbitrary")),
        name="mod",
    )(cvec, w_mod, b_mod.reshape(DEPTH, 1, 6 * D))


def _mod_row_map(rows_per_cond, tm, base):
    return lambda i: (base + (i * tm) // rows_per_cond, 0, 0)


def _inproj_kernel(x_ref, mod_ref, w_ref, o_ref):
    m = mod_ref[0]
    h = x_ref[...] * (1.0 + m[:, D:2 * D]) + m[:, 0:D]
    o_ref[...] = jnp.dot(h.astype(BF16), w_ref[...], preferred_element_type=F32)


def _inproj(x, mod8, w_in_p, rows_per_cond, base):
    n = x.shape[0]
    tm, tn = min(n, 2048), 1152
    rmap = _mod_row_map(rows_per_cond, tm, base)
    return pl.pallas_call(
        _inproj_kernel,
        grid=(PC // tn, n // tm),
        in_specs=[pl.BlockSpec((tm, D), lambda c, i: (i, 0)),
                  pl.BlockSpec((1, 1, 6 * D), lambda c, i: rmap(i)),
                  pl.BlockSpec((D, tn), lambda c, i: (0, c))],
        out_specs=pl.BlockSpec((tm, tn), lambda c, i: (i, c)),
        out_shape=jax.ShapeDtypeStruct((n, PC), F32),
        compiler_params=_cparams(("arbitrary", "arbitrary")),
        name="inproj",
    )(x, mod8, w_in_p)


def _outmlp_kernel(yh, hf, hb, og, mlg, yf, yb, bon, rg, lng, lnb, ya, x_ref, mod_ref,
                   wo, g1, b1, w1, w2, g2, b2, o_ref):
    m = mod_ref[0]
    y_ml = _head_norm(hf[...] + hb[...], 1e-6) * mlg[...] * jax.nn.sigmoid(og[...])
    y_rw = (_head_norm(yf[...] + yb[...], RW_GN_EPS) * lng[...] + lnb[...] + bon[...]) * rg[...]
    y = jnp.concatenate([yh[...], y_ml, y_rw, ya[...]], axis=-1).astype(BF16)
    mix = jnp.dot(y, wo[...], preferred_element_type=F32)
    x1 = _layernorm(ALPHA * x_ref[...] + m[:, 2 * D:3 * D] * mix, g1[...], b1[...])
    h = (x1 * (1.0 + m[:, 4 * D:5 * D]) + m[:, 3 * D:4 * D]).astype(BF16)
    acc = jnp.zeros(x1.shape, F32)
    for c in range(D_FF // D):
        a = jnp.dot(h, w1[:, c * D:(c + 1) * D], preferred_element_type=F32)
        a = jnp.square(jnp.maximum(a, 0.0)).astype(BF16)
        acc = acc + jnp.dot(a, w2[c * D:(c + 1) * D, :], preferred_element_type=F32)
    o_ref[...] = _layernorm(ALPHA * x1 + m[:, 5 * D:6 * D] * acc, g2[...], b2[...])


def _outmlp(ys, x, mod8, wo, g1, b1, w1, w2, g2, b2, rows_per_cond, base):
    n = x.shape[0]
    tm = 512
    rmap = _mod_row_map(rows_per_cond, tm, base)
    const = lambda shape: pl.BlockSpec(shape, lambda i: (0, 0), pipeline_mode=pl.Buffered(1))
    row = lambda wd: pl.BlockSpec((tm, wd), lambda i: (i, 0))
    vec = pl.BlockSpec((1, W), lambda i: (0, 0))
    return pl.pallas_call(
        _outmlp_kernel,
        grid=(n // tm,),
        in_specs=[row(W), row(W), row(W), pl.BlockSpec((tm, W), lambda i: (i, C_ML // W + 3)), vec,
                  row(W), row(W), row(W), row(W), vec, vec, row(W), row(D),
                  pl.BlockSpec((1, 1, 6 * D), lambda i: rmap(i)),
                  const((D, D)), const((1, D)), const((1, D)),
                  const((D, D_FF)), const((D_FF, D)), const((1, D)), const((1, D))],
        out_specs=row(D),
        out_shape=jax.ShapeDtypeStruct((n, D), F32),
        compiler_params=_cparams(("arbitrary",)),
        name="outmlp",
    )(*ys, x, mod8, wo, g1, b1, w1, w2, g2, b2)


def _hyfilt_kernel(z_ref, w1, b1, fr, w2, b2, w3, dec_ref, h_ref, nrm_ref):
    i = pl.program_id(0)
    f = fr[...]
    hid = jnp.sin(f * (_mm(z_ref[...], w1[...]) + b1[...]))
    hid = jnp.sin(f * (_mm(hid, w2[...]) + b2[...]))
    h = _mm(hid, w3[...])
    dec = dec_ref[...]
    h = h * jnp.concatenate([dec, dec, dec, dec], axis=1)
    rid = lax.broadcasted_iota(jnp.int32, h.shape, 0)
    cid = lax.broadcasted_iota(jnp.int32, h.shape, 1)
    anti = ((cid >> 8) & 1) == 1
    h = jnp.where(anti & (rid == 0) & (i == 0), 0.0, h)
    h_ref[...] = h.astype(BF16)
    part = jnp.broadcast_to(jnp.sum(jnp.abs(h), axis=0, keepdims=True), (8, 4 * W))

    @pl.when(i == 0)
    def _():
        nrm_ref[...] = part

    @pl.when(i > 0)
    def _():
        nrm_ref[...] = nrm_ref[...] + part


def _hyfilt(L, z, w1p, b1, fr, w2, b2, w3, dec):
    tl = min(L, 512)
    const = lambda shape: pl.BlockSpec(shape, lambda i: (0, 0))
    return pl.pallas_call(
        _hyfilt_kernel,
        grid=(L // tl,),
        in_specs=[pl.BlockSpec((tl, 128), lambda i: (i, 0)),
                  const((128, HY_FFN)), const((1, HY_FFN)), const((1, HY_FFN)),
                  const((HY_FFN, HY_FFN)), const((1, HY_FFN)), const((HY_FFN, 4 * W)),
                  pl.BlockSpec((tl, W), lambda i: (i, 0))],
        out_specs=[pl.BlockSpec((tl, 4 * W), lambda i: (i, 0)), const((8, 4 * W))],
        out_shape=[jax.ShapeDtypeStruct((L, 4 * W), BF16), jax.ShapeDtypeStruct((8, 4 * W), F32)],
        compiler_params=_cparams(("arbitrary",)),
        name="hyfilt",
    )(z, w1p, b1, fr, w2, b2, w3, dec)


def _hyspec_kernel(fc, fs, h_ref, nrm_ref, re_ref, im_ref, *, L):
    k = pl.program_id(0)
    h = h_ref[...]
    re = jnp.dot(fc[...], h, preferred_element_type=F32)
    im = jnp.dot(fs[...], h, preferred_element_type=F32)
    tk = re.shape[0]
    rid = lax.broadcasted_iota(jnp.int32, (tk, W), 0) + k * tk
    packed = rid == 0
    scale = jnp.where(packed, 1.0, 2.0) * (1.0 / (2 * L))
    nrm = nrm_ref[0:1, :]
    res, ims = [], []
    for f in range(2):
        c0, c1 = 2 * f * W, (2 * f + 1) * W
        inv = scale / (nrm[:, c0:c0 + W] + nrm[:, c1:c1 + W])
        res.append((re[:, c0:c0 + W] + re[:, c1:c1 + W]) * inv)
        ims.append(jnp.where(packed, im[:, c0:c0 + W] + im[:, c1:c1 + W],
                             im[:, c0:c0 + W] - im[:, c1:c1 + W]) * inv)
    re_ref[...] = jnp.concatenate(res, axis=1)
    im_ref[...] = jnp.concatenate(ims, axis=1)


def _hyspec(L, fc, fs, h, nrm):
    tk = min(L, 512)
    return pl.pallas_call(
        functools.partial(_hyspec_kernel, L=L),
        grid=(L // tk,),
        in_specs=[pl.BlockSpec((tk, L), lambda k: (k, 0)),
                  pl.BlockSpec((tk, L), lambda k: (k, 0)),
                  pl.BlockSpec((L, 4 * W), lambda k: (0, 0), pipeline_mode=pl.Buffered(1)),
                  pl.BlockSpec((8, 4 * W), lambda k: (0, 0))],
        out_specs=[pl.BlockSpec((tk, 2 * W), lambda k: (k, 0))] * 2,
        out_shape=[jax.ShapeDtypeStruct((L, 2 * W), F32)] * 2,
        compiler_params=_cparams(("arbitrary",)),
        name="hyspec",
    )(fc, fs, h, nrm)


def _seqs_per_step(B, L):
    return max(1, min(B, 2048 // L))


def _hypre_kernel(x_ref, w_ref, o_ref, *, L):
    x = x_ref[...]
    R = x.shape[0]
    w = w_ref[0]
    pos = lax.broadcasted_iota(jnp.int32, x.shape, 0) & (L - 1)
    xp = jnp.where(pos == 0, 0.0, pltpu.roll(x, 1, axis=0))
    xn = jnp.where(pos == L - 1, 0.0, pltpu.roll(x, R - 1, axis=0))
    o_ref[...] = xp * w[0:1, :] + x * w[1:2, :] + xn * w[2:3, :]


def _hypre(p, B, L, conv_w3):
    assert L & (L - 1) == 0
    nb = _seqs_per_step(B, L)
    return pl.pallas_call(
        functools.partial(_hypre_kernel, L=L),
        grid=(B // nb, 3),
        in_specs=[pl.BlockSpec((nb * L, W), lambda b, c: (b, c)),
                  pl.BlockSpec((1, 3, W), lambda b, c: (c, 0, 0))],
        out_specs=pl.BlockSpec((nb * L, W), lambda b, c: (b, c)),
        out_shape=jax.ShapeDtypeStruct((B * L, 3 * W), F32),
        compiler_params=_cparams(("arbitrary", "arbitrary")),
        name="hypre",
    )(p, conv_w3)


def _hyfwd_kernel(u_ref, fc, fs, fre_ref, fim_ref, yre_ref, yim_ref, ub, *, L, nb):
    k = pl.program_id(1)

    @pl.when(k == 0)
    def _():
        ub[...] = u_ref[...].astype(BF16)

    fre, fim = fre_ref[...], fim_ref[...]
    tk = fre.shape[0]
    packed = (lax.broadcasted_iota(jnp.int32, fre.shape, 0) + k * tk) == 0
    for s in range(nb):
        u = ub[s * L:(s + 1) * L, :]
        re = jnp.dot(fc[...], u, preferred_element_type=F32)
        im = jnp.dot(fs[...], u, preferred_element_type=F32)
        yre_ref[s * tk:(s + 1) * tk, :] = jnp.where(packed, re * fre, re * fre - im * fim)
        yim_ref[s * tk:(s + 1) * tk, :] = jnp.where(packed, im * fim, re * fim + im * fre)


def _hyfwd(u, ucol, B, L, fc, fs, fre, fim, f):
    tk = min(L, 1024)
    nb = _seqs_per_step(B, L) if tk == L else 1
    nk = L // tk
    return pl.pallas_call(
        functools.partial(_hyfwd_kernel, L=L, nb=nb),
        grid=(B // nb, nk),
        in_specs=[pl.BlockSpec((nb * L, W), lambda b, k: (b, ucol)),
                  pl.BlockSpec((tk, L), lambda b, k: (k, 0)),
                  pl.BlockSpec((tk, L), lambda b, k: (k, 0)),
                  pl.BlockSpec((tk, W), lambda b, k: (k, f)),
                  pl.BlockSpec((tk, W), lambda b, k: (k, f))],
        out_specs=[pl.BlockSpec((nb * tk, W), lambda b, k: (b * nk + k, 0))] * 2,
        out_shape=[jax.ShapeDtypeStruct((B * L, W), F32)] * 2,
        scratch_shapes=[pltpu.VMEM((nb * L, W), BF16)],
        compiler_params=_cparams(("arbitrary", "arbitrary")),
        name="hyfwd",
    )(u, fc, fs, fre, fim)


def _hyinv_kernel(yre_ref, yim_ref, fc, fst, u_ref, g_ref, bias_ref, o_ref, yb, *, L, nb):
    t = pl.program_id(1)

    @pl.when(t == 0)
    def _():
        yb[0] = yre_ref[...].astype(BF16)
        yb[1] = yim_ref[...].astype(BF16)

    tt = fc.shape[0]
    for s in range(nb):
        y = jnp.dot(fc[...], yb[0, s * L:(s + 1) * L, :], preferred_element_type=F32) \
            + jnp.dot(fst[...], yb[1, s * L:(s + 1) * L, :], preferred_element_type=F32)
        rows = slice(s * tt, (s + 1) * tt)
        o_ref[rows, :] = g_ref[rows, :] * (y + u_ref[rows, :] * bias_ref[0])


def _hyinv(yre, yim, B, L, fc, fst, u, ucol, gate, gcol, bias, f):
    tt = min(L, 512)
    nt = L // tt
    nb = _seqs_per_step(B, L) if tt == L else 1
    return pl.pallas_call(
        functools.partial(_hyinv_kernel, L=L, nb=nb),
        grid=(B // nb, nt),
        in_specs=[pl.BlockSpec((nb * L, W), lambda b, t: (b, 0)),
                  pl.BlockSpec((nb * L, W), lambda b, t: (b, 0)),
                  pl.BlockSpec((tt, L), lambda b, t: (t, 0)),
                  pl.BlockSpec((tt, L), lambda b, t: (t, 0)),
                  pl.BlockSpec((nb * tt, W), lambda b, t: (b * nt + t, ucol)),
                  pl.BlockSpec((nb * tt, W), lambda b, t: (b * nt + t, gcol)),
                  pl.BlockSpec((1, 1, W), lambda b, t: (f, 0, 0))],
        out_specs=pl.BlockSpec((nb * tt, W), lambda b, t: (b * nt + t, 0)),
        out_shape=jax.ShapeDtypeStruct((B * L, W), F32),
        scratch_shapes=[pltpu.VMEM((2, nb * L, W), BF16)],
        compiler_params=_cparams(("arbitrary", "arbitrary")),
        name="hyinv",
    )(yre, yim, fc, fst, u, gate, bias)


def _dft_consts(L):
    s = int(round(math.sqrt(L)))
    assert s * s == L
    k = jnp.arange(L, dtype=jnp.int32)[:, None]
    n1 = jnp.arange(s, dtype=jnp.int32)[None, :]
    a1 = ((k * (n1 * s)) % (2 * L)).astype(F32) * (math.pi / L)
    a2 = ((k * n1) % (2 * L)).astype(F32) * (math.pi / L)
    c1, s1, c2, s2 = jnp.cos(a1), jnp.sin(a1), jnp.cos(a2), jnp.sin(a2)
    fc = (c1[:, :, None] * c2[:, None, :] - s1[:, :, None] * s2[:, None, :]).reshape(L, L)
    fs = -(s1[:, :, None] * c2[:, None, :] + c1[:, :, None] * s2[:, None, :]).reshape(L, L)
    nyq = (1.0 - 2.0 * (jnp.arange(L) % 2)).astype(F32)[None, :]
    fs = jnp.where(k == 0, nyq, fs)
    return fc.astype(BF16), fs.astype(BF16), fs.T.astype(BF16)


def _hyena_consts(L):
    t = jnp.linspace(0.0, 1.0, L, dtype=F32)[:, None]
    ang = (2.0 * math.pi / L) * jnp.arange(L, dtype=F32)[:, None]
    bands = jnp.linspace(1e-4, HY_BANDS - 1, HY_BANDS, dtype=F32)[None, :]
    z = jnp.concatenate([t, jnp.cos(bands * ang), jnp.sin(bands * ang)], axis=-1)
    z = jnp.pad(z, ((0, 0), (0, 128 - HY_EMB)))
    deltas = jnp.abs(jnp.linspace(HY_MIN_DECAY, HY_MAX_DECAY, W, dtype=F32))
    dec = jnp.exp(-t * deltas)
    return z, dec


def _hyena(p, B, L, hc, lp):
    z, dec, fc, fs, fst = hc
    w1p = jnp.pad(lp['hy_w1'], ((0, 128 - HY_EMB), (0, 0)))
    row = lambda a: a.reshape(1, -1)
    h, nrm = _hyfilt(L, z, w1p, row(lp['hy_b1']), row(lp['hy_freq']), lp['hy_w2'], row(lp['hy_b2']),
                     lp['hy_w3'], dec)
    fre, fim = _hyspec(L, fc, fs, h, nrm)
    pc = _hypre(p, B, L, lp['hy_conv'].reshape(3, 3, W).transpose(1, 0, 2))
    bias = lp['hy_bias'].reshape(2, 1, W)
    yre, yim = _hyfwd(pc, 0, B, L, fc, fs, fre, fim, 0)
    zmid = _hyinv(yre, yim, B, L, fc, fst, pc, 0, pc, 1, bias, 0)
    yre, yim = _hyfwd(zmid, 0, B, L, fc, fs, fre, fim, 1)
    return _hyinv(yre, yim, B, L, fc, fst, zmid, 0, pc, 2, bias, 1)


def _mlstm_kernel(qf, kf, vf, gf, qb, kb, vb, gb, gbias, c0, n0, m0,
                  hf_o, hb_o, c_o, n_o, m_o, c_s, n_s, m_s, *, nc, nb):
    j = pl.program_id(1)
    T = ML_CHUNK

    @pl.when(j == 0)
    def _():
        c_s[...] = c0[...]
        n_s[...] = n0[...]
        m_s[...] = m0[...]

    lane = lax.broadcasted_iota(jnp.int32, (T, 128), 1)
    is_f = ((lane & 4) != 0) & (lane < 16)
    row = lax.broadcasted_iota(jnp.int32, (T, T), 0)
    col = lax.broadcasted_iota(jnp.int32, (T, T), 1)
    masks = (row >= col, row <= col)
    refs = ((qf, kf, vf, gf), (qb, kb, vb, gb))
    grp = [(d, i) for d in range(2) for i in range(nb)]
    Gs = []
    for d, i in grp:
        g = refs[d][3][i] + gbias[...]
        Gs.append(jnp.where(is_f, jnp.minimum(g, 0.0) - jnp.log1p(jnp.exp(-jnp.abs(g))), g))
    Bcs = [_sel_mm(masks[d].astype(BF16), G) for (d, i), G in zip(grp, Gs)]
    sr = lax.broadcasted_iota(jnp.int32, (128, NH * 128), 0)
    sc = lax.broadcasted_iota(jnp.int32, (128, NH * 128), 1) >> 7
    units = []
    for gi, (d, i) in enumerate(grp):
        G, Bc = Gs[gi], Bcs[gi]
        GT, BcT = G.T, Bc.T
        Bcol = _mm_sel(Bc, (sr == sc + (d * 8 + 4)).astype(BF16), 3)
        Icol = _mm_sel(G, (sr == sc + d * 8).astype(BF16), 3)
        q32 = refs[d][0][i] * (HD ** -0.5)
        q = q32.astype(BF16)
        k = refs[d][1][i]
        v = refs[d][2][i].astype(BF16)
        for h in range(NH):
            cf, ci = d * 8 + 4 + h, d * 8 + h
            sl = slice(h * HD, (h + 1) * HD)
            b_col = Bcol[:, h * 128:(h + 1) * 128]
            units.append(dict(
                d=d, i=i, u=d * NH + h, q=q[:, sl], q32=q32[:, sl], k=k[:, sl], v=v[:, sl],
                b_col=b_col, b_row=BcT[cf:cf + 1, :], i_col=Icol[:, h * 128:(h + 1) * 128], i_row=GT[ci:ci + 1, :],
                b_tot=b_col[T - 1:T, :] if d == 0 else b_col[0:1, :]))
    for un in units:
        un['m_prev'] = m_s[un['i'], un['u']:un['u'] + 1, :]
        un['C'] = c_s[un['i'], un['u']]
        un['n'] = n_s[un['i'], un['u']:un['u'] + 1, :]
    for un in units:
        un['qk'] = _mm_nt(un['q'], un['k'], False)
    for un in units:
        un['qC'] = _mm(un['q'], un['C'], False)
    for un in units:
        logd = jnp.where(masks[un['d']], un['b_col'] - un['b_row'] + un['i_row'], -jnp.inf)
        inter = un['b_col'] + un['m_prev']
        m_t = jnp.maximum(inter, jnp.max(logd, axis=-1, keepdims=True))
        un['s'] = un['qk'] * jnp.exp(logd - m_t)
        un['w_in'] = jnp.exp(inter - m_t)
        un['floor'] = jnp.exp(-m_t)
        logw = un['b_tot'] - un['b_col'] + un['i_col']
        m_new = jnp.maximum(un['b_tot'] + un['m_prev'], jnp.max(logw, axis=0, keepdims=True))
        un['kw'] = un['k'] * jnp.exp(logw - m_new)[:, 0:HD]
        un['decay'] = jnp.exp(un['b_tot'] + un['m_prev'] - m_new)[:, 0:HD]
        un['m_new'] = m_new
    for un in units:
        un['sv'] = _mm(un['s'], un['v'], False)
    for un in units:
        un['kv'] = _mm(un['kw'].T, un['v'], False)
    hs = {}
    for un in units:
        i, u = un['i'], un['u']
        num = un['sv'] + un['w_in'][:, 0:HD] * un['qC']
        den = jnp.sum(un['s'], axis=-1, keepdims=True) \
            + un['w_in'] * jnp.sum(un['q32'] * un['n'], axis=-1, keepdims=True)
        hs.setdefault((un['d'], i), []).append(num / jnp.maximum(jnp.abs(den), un['floor'])[:, 0:HD])
        c_s[i, u] = un['decay'] * un['C'] + un['kv']
        n_s[i, u:u + 1, :] = un['decay'] * un['n'] + jnp.sum(un['kw'], axis=0, keepdims=True)
        m_s[i, u:u + 1, :] = un['m_new']
    for (d, i), parts in hs.items():
        (hf_o, hb_o)[d][i] = jnp.concatenate(parts, axis=1)

    @pl.when(j == nc - 1)
    def _():
        c_o[...] = c_s[...]
        n_o[...] = n_s[...]
        m_o[...] = m_s[...]


def _mlstm_scan(p, B, L, gbias, c0, n0, m0):
    T = ML_CHUNK
    nc = L // T
    nb = 2
    p3 = p.reshape(B, L, PC)
    fwd = lambda cb: (lambda b, j: (b, j, cb))
    bwd = lambda cb: (lambda b, j: (b, nc - 1 - j, cb))
    blk = lambda f, cb, wd=W: pl.BlockSpec((nb, T, wd), f(cb))
    cq, ck, cv, cg = C_ML // W, C_ML // W + 1, C_ML // W + 2, C_MLG // 128
    st = lambda shape: pl.BlockSpec((nb,) + shape, lambda b, j: (b,) + (0,) * len(shape))
    hf, hb, C, n, m = pl.pallas_call(
        functools.partial(_mlstm_kernel, nc=nc, nb=nb),
        grid=(B // nb, nc),
        in_specs=[blk(fwd, cq), blk(fwd, ck), blk(fwd, cv), blk(fwd, cg, 128),
                  blk(bwd, cq), blk(bwd, ck), blk(bwd, cv), blk(bwd, cg, 128),
                  pl.BlockSpec((1, 128), lambda b, j: (0, 0)),
                  st((8, HD, HD)), st((8, HD)), st((8, 128))],
        out_specs=[pl.BlockSpec((nb, T, W), lambda b, j: (b, j, 0)),
                   pl.BlockSpec((nb, T, W), lambda b, j: (b, nc - 1 - j, 0)),
                   st((8, HD, HD)), st((8, HD)), st((8, 128))],
        out_shape=[jax.ShapeDtypeStruct((B, L, W), F32), jax.ShapeDtypeStruct((B, L, W), F32),
                   jax.ShapeDtypeStruct((B, 8, HD, HD), F32), jax.ShapeDtypeStruct((B, 8, HD), F32),
                   jax.ShapeDtypeStruct((B, 8, 128), F32)],
        scratch_shapes=[pltpu.VMEM((nb, 8, HD, HD), F32), pltpu.VMEM((nb, 8, HD), F32),
                        pltpu.VMEM((nb, 8, 128), F32)],
        compiler_params=_cparams(("arbitrary", "arbitrary")),
        name="mlstm",
    )(p3, p3, p3, p3, p3, p3, p3, p3, gbias, c0, n0, m0)
    return hf.reshape(B * L, W), hb.reshape(B * L, W), C, n, m


def _shifted(x, prev8, next8, first, last):
    T = x.shape[0]
    rid = lax.broadcasted_iota(jnp.int32, x.shape, 0)
    pr = jnp.where(first, 0.0, prev8[7:8, :])
    nx = jnp.where(last, 0.0, next8[0:1, :])
    xp = jnp.where(rid == 0, pr, pltpu.roll(x, 1, axis=0))
    xn = jnp.where(rid == T - 1, nx, pltpu.roll(x, T - 1, axis=0))
    return xp, xn


def _rwprep_kernel(*refs, L, T):
    xs, prevs, nexts = refs[0:4], refs[4:8], refs[8:12]
    mu, w0, w2, a0, a2, g2, kkw, kaw, rkw = refs[12:21]
    r_o, v_o, kk_o, g_o, bon_o, lw0_o, lw1_o, kt0_o, kt1_o, ba0_o, ba1_o = refs[21:]
    i = pl.program_id(0)
    first = (i * T) % L == 0
    last = ((i + 1) * T) % L == 0
    pf = []
    for c in range(4):
        x = xs[c][...]
        xp, xn = _shifted(x, prevs[c][...], nexts[c][...], first, last)
        pf.append(x + mu[:, c * W:(c + 1) * W] * (0.5 * (xp + xn) - x))
    r, k, v, misc = pf
    lw, la, lg = misc[:, 0:64], misc[:, 64:128], misc[:, 128:256]
    ones = _head_ones(W)
    g_o[...] = _mm3(jax.nn.sigmoid(lg), g2[...])
    kq = k * kkw[...]
    kk = kq / jnp.maximum(jnp.sqrt(_mm_sel(kq * kq, ones)), 1e-12)
    tw = jnp.tanh(lw)
    r_o[...] = r
    v_o[...] = v
    kk_o[...] = kk
    bon = jnp.zeros_like(r)
    for d, (lw_o, kt_o, ba_o) in enumerate(((lw0_o, kt0_o, ba0_o), (lw1_o, kt1_o, ba1_o))):
        lw_o[...] = -RW_DECAY_SCALE * jax.nn.sigmoid(w0[d:d + 1, :] + _mm3(tw, w2[d]))
        a = jax.nn.sigmoid(a0[d:d + 1, :] + _mm3(la, a2[d]))
        kt = k * (1.0 + (a - 1.0) * kaw[...])
        kt_o[...] = kt
        ba_o[...] = kk * a
        bon = bon + _mm_sel(r * kt * rkw[...], ones) * v
    bon_o[...] = bon


def _rwprep(p, B, L, lp):
    n = B * L
    T = min(L, 512)
    nb8 = n // 8
    c0 = C_RW // W
    cur = lambda c: pl.BlockSpec((T, W), lambda i: (i, c0 + c))
    prv = lambda c: pl.BlockSpec((8, W), lambda i: (jnp.maximum(i * (T // 8) - 1, 0), c0 + c))
    nxt = lambda c: pl.BlockSpec((8, W), lambda i: (jnp.minimum((i + 1) * (T // 8), nb8 - 1), c0 + c))
    const = lambda shape: pl.BlockSpec(shape, lambda i: (0,) * len(shape))
    row = lambda a: a.reshape(1, -1)
    out = pl.BlockSpec((T, W), lambda i: (i, 0))
    return pl.pallas_call(
        functools.partial(_rwprep_kernel, L=L, T=T),
        grid=(n // T,),
        in_specs=[cur(c) for c in range(4)] + [prv(c) for c in range(4)] + [nxt(c) for c in range(4)]
        + [const((1, 4 * W)), const((2, W)), const((2, 64, W)), const((2, W)), const((2, 64, W)),
           const((128, W)), const((1, W)), const((1, W)), const((1, W))],
        out_specs=[out] * 11,
        out_shape=[jax.ShapeDtypeStruct((n, W), F32)] * 11,
        compiler_params=_cparams(("arbitrary",)),
        name="rwprep",
    )(*([p] * 12), row(lp['rw_mu']), lp['rw_w0'], lp['rw_w2'], lp['rw_a0'], lp['rw_a2'], lp['rw_g2'],
      row(lp['rw_kk']), row(lp['rw_ka']), row(lp['rw_rk']))


def _rwscan_kernel(rf, vf, kkf, lwf, ktf, baf, rb, vb, kkb, lwb, ktb, bab, s0,
                   yf_o, yb_o, s_o, s_s, *, nc, nb):
    j = pl.program_id(1)
    T = RW_CHUNK

    @pl.when(j == 0)
    def _():
        s_s[...] = s0[...]

    row = lax.broadcasted_iota(jnp.int32, (T, T), 0)
    col = lax.broadcasted_iota(jnp.int32, (T, T), 1)
    incl = (row >= col, row <= col)
    strict = (row > col, row < col)
    refs = ((rf, vf, kkf, lwf, ktf, baf), (rb, vb, kkb, lwb, ktb, bab))
    grp = [(d, i) for d in range(2) for i in range(nb)]
    lws = [refs[d][3][i] for d, i in grp]
    lcs = [_sel_mm(incl[d].astype(BF16), lw) for (d, i), lw in zip(grp, lws)]
    units = []
    for (d, i), lw, lc in zip(grp, lws, lcs):
        r_ref, v_ref, kk_ref, _, kt_ref, ba_ref = refs[d]
        gam = jnp.exp(lc)
        ginv = jnp.exp(-lc)
        at = -kk_ref[i] * jnp.exp(lc - lw)
        ar = jnp.concatenate([at, r_ref[i] * gam], axis=0).astype(BF16)
        bk = jnp.concatenate([ba_ref[i] * ginv, kt_ref[i] * ginv], axis=0).astype(BF16)
        v = v_ref[i]
        vb16 = v.astype(BF16)
        gtot = gam[T - 1:T, :] if d == 0 else gam[0:1, :]
        for h in range(NH):
            sl = slice(h * HD, (h + 1) * HD)
            units.append(dict(d=d, i=i, u=d * NH + h, ar=ar[:, sl], bk=bk[:, sl], v=v[:, sl], vb=vb16[:, sl],
                              gtot=gtot[:, sl]))
    for un in units:
        un['S'] = s_s[un['i'], un['u']]
    for un in units:
        un['P'] = lax.dot_general(un['ar'], un['bk'], _NT, preferred_element_type=F32)
    for un in units:
        un['AS'] = _mm_nt(un['ar'], un['S'], False)
    for un in units:
        P, d = un['P'], un['d']
        un['A_ab'] = jnp.where(strict[d], P[0:T, 0:T], 0.0)
        un['A_rb'] = jnp.where(incl[d], P[T:2 * T, 0:T], 0.0).astype(BF16)
        un['A_k'] = jnp.concatenate([jnp.where(strict[d], P[0:T, T:2 * T], 0.0),
                                     jnp.where(incl[d], P[T:2 * T, T:2 * T], 0.0)], axis=0).astype(BF16)
    for un in units:
        un['AV'] = jnp.dot(un['A_k'], un['vb'], preferred_element_type=F32)
    pair = ((row >> 1) == (col >> 1)) & (row != col)
    eye = (row == col).astype(F32)
    for un in units:
        un['X'] = eye + jnp.where(pair, un['A_ab'], 0.0)
    bs = 2
    while bs < T:
        sh = bs.bit_length() - 1
        pair = ((row >> (sh + 1)) == (col >> (sh + 1))) & ((row >> sh) != (col >> sh))
        for un in units:
            un['Xb'] = un['X'].astype(BF16)
            un['XN'] = jnp.dot(un['Xb'], jnp.where(pair, un['A_ab'], 0.0).astype(BF16), preferred_element_type=F32)
        for un in units:
            un['X'] = un['X'] + jnp.dot(un['XN'].astype(BF16), un['Xb'], preferred_element_type=F32)
        bs *= 2
    for un in units:
        un['U'] = _mm(un['X'], un['AS'][0:T] + un['AV'][0:T], False)
    for un in units:
        un['Y'] = un['AS'][T:2 * T] + un['AV'][T:2 * T] + jnp.dot(un['A_rb'], un['U'].astype(BF16),
                                                                  preferred_element_type=F32)
    for un in units:
        UV = jnp.concatenate([un['U'], un['v']], axis=0)
        s_s[un['i'], un['u']] = (un['S'] + _mm(UV.T, un['bk'], False)) * un['gtot']
    ys = {}
    for un in units:
        ys.setdefault((un['d'], un['i']), []).append(un['Y'])
    for (d, i), parts in ys.items():
        (yf_o, yb_o)[d][i] = jnp.concatenate(parts, axis=1)

    @pl.when(j == nc - 1)
    def _():
        s_o[...] = s_s[...]


def _rwscan(prep, B, L, s0):
    T = RW_CHUNK
    nc = L // T
    nb = 2
    r, v, kk, lw0, lw1, kt0, kt1, ba0, ba1 = [a.reshape(B, L, W) for a in prep]
    fwd = pl.BlockSpec((nb, T, W), lambda b, j: (b, j, 0))
    bwd = pl.BlockSpec((nb, T, W), lambda b, j: (b, nc - 1 - j, 0))
    st = pl.BlockSpec((nb, 8, HD, HD), lambda b, j: (b, 0, 0, 0))
    yf, yb, S = pl.pallas_call(
        functools.partial(_rwscan_kernel, nc=nc, nb=nb),
        grid=(B // nb, nc),
        in_specs=[fwd] * 6 + [bwd] * 6 + [st],
        out_specs=[fwd, bwd, st],
        out_shape=[jax.ShapeDtypeStruct((B, L, W), F32), jax.ShapeDtypeStruct((B, L, W), F32),
                   jax.ShapeDtypeStruct((B, 8, HD, HD), F32)],
        scratch_shapes=[pltpu.VMEM((nb, 8, HD, HD), F32)],
        compiler_params=_cparams(("arbitrary", "arbitrary")),
        name="rwscan",
    )(r, v, kk, lw0, kt0, ba0, r, v, kk, lw1, kt1, ba1, s0)
    return yf.reshape(B * L, W), yb.reshape(B * L, W), S


def _rms(x, gain):
    ms = _mm_sel(x * x, _head_ones(x.shape[1])) * (1.0 / HD)
    return x * lax.rsqrt(ms + 1e-6) * gain


def _rope(x, cos, sin):
    wd = x.shape[1]
    lane = lax.broadcasted_iota(jnp.int32, x.shape, 1)
    partner = jnp.where((lane & 16) == 0, pltpu.roll(x, wd - 16, axis=1), pltpu.roll(x, 16, axis=1))
    return x * cos + partner * sin


def _attn_kernel(*refs, rope, cached, tq, kchunk):
    if rope:
        q_ref, k_ref, v_ref, qn, kn, cosq, sinq, cosk, sink = refs[:9]
        rest = refs[9:]
    else:
        q_ref, k_ref, v_ref, qn, kn = refs[:5]
        rest = refs[5:]
    if cached:
        kc_ref, vc_ref = rest[:2]
        rest = rest[2:]
    o_ref, kn_o, k_s, v_s = rest
    t = pl.program_id(1)
    L = k_ref.shape[0]
    Lk = k_s.shape[0]

    def with_ones(v):
        ones = jnp.ones((v.shape[0], HD), BF16)
        return jnp.concatenate([v[:, 0:HD].astype(BF16), ones, v[:, HD:2 * HD].astype(BF16), ones], axis=1)

    @pl.when(t == 0)
    def _():
        k = _rms(k_ref[...], kn[...])
        if rope:
            k = _rope(k, cosk[...], sink[...])
        kn_o[...] = k
        k_s[0:L, :] = k.astype(BF16)
        v_s[0:L, :] = with_ones(v_ref[...])
        if cached:
            k_s[L:Lk, :] = kc_ref[0].astype(BF16)
            v_s[L:Lk, :] = with_ones(vc_ref[0])

    q = _rms(q_ref[...], qn[...])
    if rope:
        q = _rope(q, cosq[...], sinq[...])
    q = (q * (HD ** -0.5 * math.log2(math.e))).astype(BF16)
    for jkv in range(2):
        sl = slice(jkv * HD, (jkv + 1) * HD)
        q2 = jnp.concatenate([q[:, (2 * jkv) * HD:(2 * jkv + 1) * HD],
                              q[:, (2 * jkv + 1) * HD:(2 * jkv + 2) * HD]], axis=0)
        m = jnp.full((2 * tq, 1), -jnp.inf, F32)
        acc = jnp.zeros((2 * tq, 2 * HD), F32)
        for c in range(0, Lk, kchunk):
            n = min(kchunk, Lk - c)
            kc = k_s[c:c + n, sl]
            vc = v_s[c:c + n, 2 * jkv * HD:2 * (jkv + 1) * HD]
            s = _mm_nt(q2, kc, hi=False)
            m_new = jnp.maximum(m, jnp.max(s, axis=-1, keepdims=True))
            pr = jnp.exp2(s - m_new)
            acc = jnp.exp2(m - m_new) * acc + jnp.dot(pr.astype(BF16), vc, preferred_element_type=F32)
            m = m_new
        o = acc[:, 0:HD] / acc[:, HD:2 * HD]
        o_ref[:, (2 * jkv) * HD:(2 * jkv + 1) * HD] = o[0:tq]
        o_ref[:, (2 * jkv + 1) * HD:(2 * jkv + 2) * HD] = o[tq:2 * tq]


def _attn(p, B, L, qn, kn, rope_tabs=None, cache=None):
    tq = min(L, 256)
    nt = L // tq
    kchunk = min(L, 1024)
    rope, cached = rope_tabs is not None, cache is not None
    cq, ck, cv = C_AT // W, (C_AT + W) // 128, (C_AT + W) // 128 + 1
    const = lambda shape: pl.BlockSpec(shape, lambda b, t: (0, 0))
    in_specs = [pl.BlockSpec((tq, W), lambda b, t: (b * nt + t, cq)),
                pl.BlockSpec((L, 128), lambda b, t: (b, ck)),
                pl.BlockSpec((L, 128), lambda b, t: (b, cv)),
                const((1, W)), const((1, 128))]
    args = [p, p, p, qn, kn]
    Lk = L
    if rope:
        cos, sin = rope_tabs
        in_specs += [pl.BlockSpec((tq, W), lambda b, t: (t, 0)), pl.BlockSpec((tq, W), lambda b, t: (t, 0)),
                     const((L, 128)), const((L, 128))]
        args += [cos, sin, cos, sin]
    if cached:
        kc, vc = cache
        P = kc.shape[1]
        Lk = L + P
        in_specs += [pl.BlockSpec((1, P, 128), lambda b, t: (b, 0, 0))] * 2
        args += [kc, vc]
    return pl.pallas_call(
        functools.partial(_attn_kernel, rope=rope, cached=cached, tq=tq, kchunk=kchunk),
        grid=(B, nt),
        in_specs=in_specs,
        out_specs=[pl.BlockSpec((tq, W), lambda b, t: (b * nt + t, 0)),
                   pl.BlockSpec((L, 128), lambda b, t: (b, 0))],
        out_shape=[jax.ShapeDtypeStruct((B * L, W), F32), jax.ShapeDtypeStruct((B * L, 128), F32)],
        scratch_shapes=[pltpu.VMEM((Lk, 128), BF16), pltpu.VMEM((Lk, 256), BF16)],
        compiler_params=_cparams(("arbitrary", "arbitrary")),
        name="attn",
    )(*args)


def _rope_tables(L):
    t = jnp.arange(L)
    pos_r = (t // GRID_W).astype(F32)[:, None]
    pos_c = (t % GRID_W).astype(F32)[:, None]
    nf = HD // 4
    inv = ROPE_THETA ** (-jnp.arange(nf, dtype=F32) / nf)[None, :]
    ar, ac = pos_r * inv, pos_c * inv
    cos = jnp.concatenate([jnp.cos(ar), jnp.cos(ar), jnp.cos(ac), jnp.cos(ac)], axis=1)
    sin = jnp.concatenate([-jnp.sin(ar), jnp.sin(ar), -jnp.sin(ac), jnp.sin(ac)], axis=1)
    return jnp.tile(cos, (1, NH)), jnp.tile(sin, (1, NH))


def _layer(x, B, L, mod8, base, rows_per_cond, lp, consts, ctx):
    p = _inproj(x, mod8, lp['w_in_p'], rows_per_cond, base)
    y_hy = _hyena(p, B, L, consts['hy'], lp)

    if ctx is None:
        c0 = jnp.zeros((B, 8, HD, HD), F32)
        n0 = jnp.zeros((B, 8, HD), F32)
        m0 = jnp.zeros((B, 8, 128), F32)
        s0 = jnp.zeros((B, 8, HD, HD), F32)
        cache = None
    else:
        kc, vc, c0, n0, m0, s0 = ctx
        P = kc.shape[1]
        cache = (kc.reshape(B, P, 128), vc.reshape(B, P, 128))
        c0 = c0.reshape(B, 8, HD, HD)
        n0 = n0.reshape(B, 8, HD)
        m0 = jnp.broadcast_to(m0.reshape(B, 8, 1), (B, 8, 128))
        s0 = s0.reshape(B, 8, HD, HD)

    gbias = jnp.pad(lp['ml_gate_b'], (0, 128 - 4 * NH)).reshape(1, 128)
    hf, hb, C, n, m = _mlstm_scan(p, B, L, gbias, c0, n0, m0)

    r, v, kk, g, bon, lw0, lw1, kt0, kt1, ba0, ba1 = _rwprep(p, B, L, lp)
    yf, yb, S = _rwscan((r, v, kk, lw0, lw1, kt0, kt1, ba0, ba1), B, L, s0)

    qn = jnp.tile(lp['at_qn'], NH).reshape(1, W)
    kn = jnp.tile(lp['at_kn'], 2).reshape(1, 128)
    y_at, k_new = _attn(p, B, L, qn, kn, consts.get('rope'), cache)
    v_new = p[:, C_AT + W + 128:C_AT + W + 256]

    row = lambda a: a.reshape(1, D)
    vec = lambda a: a.reshape(1, W)
    ys = (y_hy, hf, hb, p, vec(lp['ml_norm_g']), yf, yb, bon, g, vec(lp['rw_ln_g']), vec(lp['rw_ln_b']), y_at)
    x = _outmlp(ys, x, mod8, lp['w_out_b'], row(lp['ln1_g']), row(lp['ln1_b']),
                lp['mlp_w1_b'], lp['mlp_w2_b'], row(lp['ln2_g']), row(lp['ln2_b']), rows_per_cond, base)
    state = (k_new.reshape(B, L, 2, HD), v_new.reshape(B, L, 2, HD), C.reshape(B, 2, NH, HD, HD),
             n.reshape(B, 2, NH, HD), m[:, :, 0].reshape(B, 2, NH), S.reshape(B, 2, NH, HD, HD))
    return x, state


def _permute_w_in(w_in):
    hy = w_in[:, :, 0:768]
    ml = w_in[:, :, 768:1792]
    mlg = w_in[:, :, 1792:1808]
    rw = w_in[:, :, 1808:2832]
    at = w_in[:, :, 2832:3344]
    mlg = jnp.pad(mlg, ((0, 0), (0, 0), (0, 128 - 16)))
    return jnp.concatenate([hy, ml, rw, at, mlg], axis=-1).astype(BF16)


def kernel(x_prompt, x_sample, cache_attn_k, cache_attn_v, state_mlstm_C, state_mlstm_n, state_mlstm_m, state_rwkv_S, c, c_ctx, w_mod, b_mod, w_in, hy_conv, hy_w1, hy_b1, hy_freq, hy_w2, hy_b2, hy_w3, hy_bias, ml_gate_b, ml_norm_g, rw_mu, rw_w0, rw_w2, rw_a0, rw_a2, rw_g2, rw_kk, rw_ka, rw_rk, rw_ln_g, rw_ln_b, at_qn, at_kn, w_out, ln1_g, ln1_b, mlp_w1, mlp_w2, ln2_g, ln2_b):
    Bp, Lp, _ = x_prompt.shape
    Bs, Ls, _ = x_sample.shape
    params = {
        'w_in_p': _permute_w_in(w_in), 'hy_conv': hy_conv, 'hy_w1': hy_w1, 'hy_b1': hy_b1, 'hy_freq': hy_freq,
        'hy_w2': hy_w2, 'hy_b2': hy_b2, 'hy_w3': hy_w3, 'hy_bias': hy_bias,
        'ml_gate_b': ml_gate_b, 'ml_norm_g': ml_norm_g,
        'rw_mu': rw_mu, 'rw_w0': rw_w0, 'rw_w2': rw_w2, 'rw_a0': rw_a0, 'rw_a2': rw_a2, 'rw_g2': rw_g2,
        'rw_kk': rw_kk, 'rw_ka': rw_ka, 'rw_rk': rw_rk, 'rw_ln_g': rw_ln_g, 'rw_ln_b': rw_ln_b,
        'at_qn': at_qn, 'at_kn': at_kn, 'w_out_b': w_out.astype(BF16), 'ln1_g': ln1_g, 'ln1_b': ln1_b,
        'mlp_w1_b': mlp_w1.astype(BF16), 'mlp_w2_b': mlp_w2.astype(BF16), 'ln2_g': ln2_g, 'ln2_b': ln2_b,
    }
    cvec = jnp.concatenate([c_ctx[None, :], c, jnp.zeros((8 - 1 - Bs, D), F32)], axis=0)
    mod = _mod_all(cvec, w_mod, b_mod).reshape(DEPTH, 8, 1, 6 * D)

    consts_p = {'hy': _hyena_consts(Lp) + _dft_consts(Lp)}
    consts_s = {'hy': _hyena_consts(Ls) + _dft_consts(Ls), 'rope': _rope_tables(Ls)}

    x = x_prompt.reshape(Bp * Lp, D)
    states = []
    for l in range(DEPTH):
        lp = {name: arr[l] for name, arr in params.items()}
        x, st = _layer(x, Bp, Lp, mod[l], 0, Bp * Lp, lp, consts_p, None)
        states.append(st)
    y_prompt = x.reshape(Bp, Lp, D)
    outs = [jnp.stack([st[i] for st in states], axis=1) for i in range(6)]

    x = x_sample.reshape(Bs * Ls, D)
    for l in range(DEPTH):
        lp = {name: arr[l] for name, arr in params.items()}
        ctx = (cache_attn_k[:, l], cache_attn_v[:, l], state_mlstm_C[:, l], state_mlstm_n[:, l],
               state_mlstm_m[:, l], state_rwkv_S[:, l])
        x, _ = _layer(x, Bs, Ls, mod[l], 1, Ls, lp, consts_s, ctx)
    y_sample = x.reshape(Bs, Ls, D)
    return (y_prompt, y_sample) + tuple(outs)
```

```python
import functools
import math

import jax
import jax.numpy as jnp
from jax import lax
from jax.experimental import pallas as pl
from jax.experimental.pallas import tpu as pltpu

F32 = jnp.float32
BF16 = jnp.bfloat16
HI = lax.Precision.HIGHEST

D = 1024
DEPTH = 4
W = 256
HD = 64
NH = 4
GRID_W = 64
D_FF = 4 * D
ALPHA = (2.0 * DEPTH) ** 0.25
HY_BANDS = 16
HY_EMB = 2 * HY_BANDS + 1
HY_FFN = 64
HY_MAX_DECAY = math.log(1e-2) / 0.3
HY_MIN_DECAY = math.log(1e-2) / 1.5
ML_CHUNK = 128
RW_CHUNK = 64
RW_DECAY_SCALE = 0.606531
RW_GN_EPS = 64e-5
ROPE_THETA = 10000.0

PC = 3456
C_HY = 0
C_ML = 768
C_RW = 1792
C_RWX = 2560
C_AT = 2816
C_MLG = 3328

VMEM_LIMIT = 56 * 1024 * 1024


def _cparams(sem):
    return pltpu.CompilerParams(dimension_semantics=sem, vmem_limit_bytes=VMEM_LIMIT)


_NN = (((1,), (0,)), ((), ()))
_NT = (((1,), (1,)), ((), ()))


def _dot(a, b, dn, hi):
    if hi:
        return lax.dot_general(a, b, dn, preferred_element_type=F32, precision=HI)
    return lax.dot_general(a.astype(BF16), b.astype(BF16), dn, preferred_element_type=F32)


def _mm(a, b, hi=True):
    return _dot(a, b, _NN, hi)


def _mm_nt(a, b, hi=True):
    return _dot(a, b, _NT, hi)


def _pieces(x, n):
    out = []
    for _ in range(n - 1):
        p = x.astype(BF16)
        out.append(p)
        x = x - p.astype(F32)
    out.append(x.astype(BF16))
    return out


def _mm3(a, b):
    ah, al = _pieces(a, 2)
    bh, bl = _pieces(b, 2)
    d = lambda x, y: jnp.dot(x, y, preferred_element_type=F32)
    return d(ah, bh) + (d(ah, bl) + d(al, bh))


def _sel_mm(sel, x, n=3):
    acc = None
    for p in _pieces(x, n):
        t = jnp.dot(sel, p, preferred_element_type=F32)
        acc = t if acc is None else acc + t
    return acc


def _mm_sel(x, sel, n=2):
    acc = None
    for p in _pieces(x, n):
        t = jnp.dot(p, sel, preferred_element_type=F32)
        acc = t if acc is None else acc + t
    return acc


def _head_ones(width):
    r = lax.broadcasted_iota(jnp.int32, (width, width), 0) >> 6
    c = lax.broadcasted_iota(jnp.int32, (width, width), 1) >> 6
    return (r == c).astype(BF16)


def _layernorm(x, g, b):
    mu = jnp.mean(x, -1, keepdims=True)
    xc = x - mu
    var = jnp.mean(xc * xc, -1, keepdims=True)
    return xc * lax.rsqrt(var + 1e-5) * g + b


def _head_norm(x, eps):
    ones = _head_ones(x.shape[1])
    mu = _mm_sel(x, ones) * (1.0 / HD)
    xc = x - mu
    var = _mm_sel(xc * xc, ones) * (1.0 / HD)
    return xc * lax.rsqrt(var + eps)


def _mod_kernel(c_ref, w_ref, b_ref, o_ref):
    c = c_ref[...]
    s = c * jax.nn.sigmoid(c)
    o_ref[0] = _mm(s, w_ref[0]) + b_ref[0]


def _mod_all(cvec, w_mod, b_mod):
    tn = 1536
    return pl.pallas_call(
        _mod_kernel,
        grid=(DEPTH, 6 * D // tn),
        in_specs=[pl.BlockSpec((8, D), lambda l, n: (0, 0)),
                  pl.BlockSpec((1, D, tn), lambda l, n: (l, 0, n)),
                  pl.BlockSpec((1, 1, tn), lambda l, n: (l, 0, n))],
        out_specs=pl.BlockSpec((1, 8, tn), lambda l, n: (l, 0, n)),
        out_shape=jax.ShapeDtypeStruct((DEPTH, 8, 6 * D), F32),
        compiler_params=_cparams(("arbitrary", "arbitrary")),
        name="mod",
    )(cvec, w_mod, b_mod.reshape(DEPTH, 1, 6 * D))


def _mod_row_map(rows_per_cond, tm, base):
    return lambda i: (base + (i * tm) // rows_per_cond, 0, 0)


def _inproj_kernel(x_ref, mod_ref, w_ref, o_ref):
    m = mod_ref[0]
    h = x_ref[...] * (1.0 + m[:, D:2 * D]) + m[:, 0:D]
    o_ref[...] = jnp.dot(h.astype(BF16), w_ref[...], preferred_element_type=F32)


def _inproj(x, mod8, w_in_p, rows_per_cond, base):
    n = x.shape[0]
    tm, tn = min(n, 2048), 1152
    rmap = _mod_row_map(rows_per_cond, tm, base)
    return pl.pallas_call(
        _inproj_kernel,
        grid=(PC // tn, n // tm),
        in_specs=[pl.BlockSpec((tm, D), lambda c, i: (i, 0)),
                  pl.BlockSpec((1, 1, 6 * D), lambda c, i: rmap(i)),
                  pl.BlockSpec((D, tn), lambda c, i: (0, c))],
        out_specs=pl.BlockSpec((tm, tn), lambda c, i: (i, c)),
        out_shape=jax.ShapeDtypeStruct((n, PC), F32),
        compiler_params=_cparams(("arbitrary", "arbitrary")),
        name="inproj",
    )(x, mod8, w_in_p)


def _outmlp_kernel(yh, hf, hb, og, mlg, yf, yb, bon, rg, lng, lnb, ya, x_ref, mod_ref,
                   wo, g1, b1, w1, w2, g2, b2, o_ref):
    m = mod_ref[0]
    y_ml = _head_norm(hf[...] + hb[...], 1e-6) * mlg[...] * jax.nn.sigmoid(og[...])
    y_rw = (_head_norm(yf[...] + yb[...], RW_GN_EPS) * lng[...] + lnb[...] + bon[...]) * rg[...]
    y = jnp.concatenate([yh[...], y_ml, y_rw, ya[...]], axis=-1).astype(BF16)
    mix = jnp.dot(y, wo[...], preferred_element_type=F32)
    x1 = _layernorm(ALPHA * x_ref[...] + m[:, 2 * D:3 * D] * mix, g1[...], b1[...])
    h = (x1 * (1.0 + m[:, 4 * D:5 * D]) + m[:, 3 * D:4 * D]).astype(BF16)
    acc = jnp.zeros(x1.shape, F32)
    for c in range(D_FF // D):
        a = jnp.dot(h, w1[:, c * D:(c + 1) * D], preferred_element_type=F32)
        a = jnp.square(jnp.maximum(a, 0.0)).astype(BF16)
        acc = acc + jnp.dot(a, w2[c * D:(c + 1) * D, :], preferred_element_type=F32)
    o_ref[...] = _layernorm(ALPHA * x1 + m[:, 5 * D:6 * D] * acc, g2[...], b2[...])


def _outmlp(ys, x, mod8, wo, g1, b1, w1, w2, g2, b2, rows_per_cond, base):
    n = x.shape[0]
    tm = 512
    rmap = _mod_row_map(rows_per_cond, tm, base)
    const = lambda shape: pl.BlockSpec(shape, lambda i: (0, 0), pipeline_mode=pl.Buffered(1))
    row = lambda wd: pl.BlockSpec((tm, wd), lambda i: (i, 0))
    vec = pl.BlockSpec((1, W), lambda i: (0, 0))
    return pl.pallas_call(
        _outmlp_kernel,
        grid=(n // tm,),
        in_specs=[row(W), row(W), row(W), pl.BlockSpec((tm, W), lambda i: (i, C_ML // W + 3)), vec,
                  row(W), row(W), row(W), row(W), vec, vec, row(W), row(D),
                  pl.BlockSpec((1, 1, 6 * D), lambda i: rmap(i)),
                  const((D, D)), const((1, D)), const((1, D)),
                  const((D, D_FF)), const((D_FF, D)), const((1, D)), const((1, D))],
        out_specs=row(D),
        out_shape=jax.ShapeDtypeStruct((n, D), F32),
        compiler_params=_cparams(("arbitrary",)),
        name="outmlp",
    )(*ys, x, mod8, wo, g1, b1, w1, w2, g2, b2)


def _hyfilt_kernel(z_ref, w1, b1, fr, w2, b2, w3, dec_ref, h_ref, nrm_ref):
    i = pl.program_id(0)
    f = fr[...]
    hid = jnp.sin(f * (_mm(z_ref[...], w1[...]) + b1[...]))
    hid = jnp.sin(f * (_mm(hid, w2[...]) + b2[...]))
    h = _mm(hid, w3[...])
    dec = dec_ref[...]
    h = h * jnp.concatenate([dec, dec, dec, dec], axis=1)
    rid = lax.broadcasted_iota(jnp.int32, h.shape, 0)
    cid = lax.broadcasted_iota(jnp.int32, h.shape, 1)
    anti = ((cid >> 8) & 1) == 1
    h = jnp.where(anti & (rid == 0) & (i == 0), 0.0, h)
    h_ref[...] = h.astype(BF16)
    part = jnp.broadcast_to(jnp.sum(jnp.abs(h), axis=0, keepdims=True), (8, 4 * W))

    @pl.when(i == 0)
    def _():
        nrm_ref[...] = part

    @pl.when(i > 0)
    def _():
        nrm_ref[...] = nrm_ref[...] + part


def _hyfilt(L, z, w1p, b1, fr, w2, b2, w3, dec):
    tl = min(L, 512)
    const = lambda shape: pl.BlockSpec(shape, lambda i: (0, 0))
    return pl.pallas_call(
        _hyfilt_kernel,
        grid=(L // tl,),
        in_specs=[pl.BlockSpec((tl, 128), lambda i: (i, 0)),
                  const((128, HY_FFN)), const((1, HY_FFN)), const((1, HY_FFN)),
                  const((HY_FFN, HY_FFN)), const((1, HY_FFN)), const((HY_FFN, 4 * W)),
                  pl.BlockSpec((tl, W), lambda i: (i, 0))],
        out_specs=[pl.BlockSpec((tl, 4 * W), lambda i: (i, 0)), const((8, 4 * W))],
        out_shape=[jax.ShapeDtypeStruct((L, 4 * W), BF16), jax.ShapeDtypeStruct((8, 4 * W), F32)],
        compiler_params=_cparams(("arbitrary",)),
        name="hyfilt",
    )(z, w1p, b1, fr, w2, b2, w3, dec)


def _hyspec_kernel(fc, fs, h_ref, nrm_ref, re_ref, im_ref, *, L):
    k = pl.program_id(0)
    h = h_ref[...]
    re = jnp.dot(fc[...], h, preferred_element_type=F32)
    im = jnp.dot(fs[...], h, preferred_element_type=F32)
    tk = re.shape[0]
    rid = lax.broadcasted_iota(jnp.int32, (tk, W), 0) + k * tk
    packed = rid == 0
    scale = jnp.where(packed, 1.0, 2.0) * (1.0 / (2 * L))
    nrm = nrm_ref[0:1, :]
    res, ims = [], []
    for f in range(2):
        c0, c1 = 2 * f * W, (2 * f + 1) * W
        inv = scale / (nrm[:, c0:c0 + W] + nrm[:, c1:c1 + W])
        res.append((re[:, c0:c0 + W] + re[:, c1:c1 + W]) * inv)
        ims.append(jnp.where(packed, im[:, c0:c0 + W] + im[:, c1:c1 + W],
                             im[:, c0:c0 + W] - im[:, c1:c1 + W]) * inv)
    re_ref[...] = jnp.concatenate(res, axis=1)
    im_ref[...] = jnp.concatenate(ims, axis=1)


def _hyspec(L, fc, fs, h, nrm):
    tk = min(L, 512)
    return pl.pallas_call(
        functools.partial(_hyspec_kernel, L=L),
        grid=(L // tk,),
        in_specs=[pl.BlockSpec((tk, L), lambda k: (k, 0)),
                  pl.BlockSpec((tk, L), lambda k: (k, 0)),
                  pl.BlockSpec((L, 4 * W), lambda k: (0, 0), pipeline_mode=pl.Buffered(1)),
                  pl.BlockSpec((8, 4 * W), lambda k: (0, 0))],
        out_specs=[pl.BlockSpec((tk, 2 * W), lambda k: (k, 0))] * 2,
        out_shape=[jax.ShapeDtypeStruct((L, 2 * W), F32)] * 2,
        compiler_params=_cparams(("arbitrary",)),
        name="hyspec",
    )(fc, fs, h, nrm)


def _seqs_per_step(B, L):
    return max(1, min(B, 2048 // L))


def _hypre_kernel(x_ref, w_ref, o_ref, *, L):
    x = x_ref[...]
    R = x.shape[0]
    w = w_ref[0]
    pos = lax.broadcasted_iota(jnp.int32, x.shape, 0) & (L - 1)
    xp = jnp.where(pos == 0, 0.0, pltpu.roll(x, 1, axis=0))
    xn = jnp.where(pos == L - 1, 0.0, pltpu.roll(x, R - 1, axis=0))
    o_ref[...] = xp * w[0:1, :] + x * w[1:2, :] + xn * w[2:3, :]


def _hypre(p, B, L, conv_w3):
    assert L & (L - 1) == 0
    nb = _seqs_per_step(B, L)
    return pl.pallas_call(
        functools.partial(_hypre_kernel, L=L),
        grid=(B // nb, 3),
        in_specs=[pl.BlockSpec((nb * L, W), lambda b, c: (b, c)),
                  pl.BlockSpec((1, 3, W), lambda b, c: (c, 0, 0))],
        out_specs=pl.BlockSpec((nb * L, W), lambda b, c: (b, c)),
        out_shape=jax.ShapeDtypeStruct((B * L, 3 * W), F32),
        compiler_params=_cparams(("arbitrary", "arbitrary")),
        name="hypre",
    )(p, conv_w3)


def _hyfwd_kernel(u_ref, fc, fs, fre_ref, fim_ref, yre_ref, yim_ref, ub, *, L, nb):
    k = pl.program_id(1)

    @pl.when(k == 0)
    def _():
        ub[...] = u_ref[...].astype(BF16)

    fre, fim = fre_ref[...], fim_ref[...]
    tk = fre.shape[0]
    packed = (lax.broadcasted_iota(jnp.int32, fre.shape, 0) + k * tk) == 0
    for s in range(nb):
        u = ub[s * L:(s + 1) * L, :]
        re = jnp.dot(fc[...], u, preferred_element_type=F32)
        im = jnp.dot(fs[...], u, preferred_element_type=F32)
        yre_ref[s * tk:(s + 1) * tk, :] = jnp.where(packed, re * fre, re * fre - im * fim)
        yim_ref[s * tk:(s + 1) * tk, :] = jnp.where(packed, im * fim, re * fim + im * fre)


def _hyfwd(u, ucol, B, L, fc, fs, fre, fim, f):
    tk = min(L, 1024)
    nb = _seqs_per_step(B, L) if tk == L else 1
    nk = L // tk
    return pl.pallas_call(
        functools.partial(_hyfwd_kernel, L=L, nb=nb),
        grid=(B // nb, nk),
        in_specs=[pl.BlockSpec((nb * L, W), lambda b, k: (b, ucol)),
                  pl.BlockSpec((tk, L), lambda b, k: (k, 0)),
                  pl.BlockSpec((tk, L), lambda b, k: (k, 0)),
                  pl.BlockSpec((tk, W), lambda b, k: (k, f)),
                  pl.BlockSpec((tk, W), lambda b, k: (k, f))],
        out_specs=[pl.BlockSpec((nb * tk, W), lambda b, k: (b * nk + k, 0))] * 2,
        out_shape=[jax.ShapeDtypeStruct((B * L, W), F32)] * 2,
        scratch_shapes=[pltpu.VMEM((nb * L, W), BF16)],
        compiler_params=_cparams(("arbitrary", "arbitrary")),
        name="hyfwd",
    )(u, fc, fs, fre, fim)


def _hyinv_kernel(yre_ref, yim_ref, fc, fst, u_ref, g_ref, bias_ref, o_ref, yb, *, L, nb):
    t = pl.program_id(1)

    @pl.when(t == 0)
    def _():
        yb[0] = yre_ref[...].astype(BF16)
        yb[1] = yim_ref[...].astype(BF16)

    tt = fc.shape[0]
    for s in range(nb):
        y = jnp.dot(fc[...], yb[0, s * L:(s + 1) * L, :], preferred_element_type=F32) \
            + jnp.dot(fst[...], yb[1, s * L:(s + 1) * L, :], preferred_element_type=F32)
        rows = slice(s * tt, (s + 1) * tt)
        o_ref[rows, :] = g_ref[rows, :] * (y + u_ref[rows, :] * bias_ref[0])


def _hyinv(yre, yim, B, L, fc, fst, u, ucol, gate, gcol, bias, f):
    tt = min(L, 512)
    nt = L // tt
    nb = _seqs_per_step(B, L) if tt == L else 1
    return pl.pallas_call(
        functools.partial(_hyinv_kernel, L=L, nb=nb),
        grid=(B // nb, nt),
        in_specs=[pl.BlockSpec((nb * L, W), lambda b, t: (b, 0)),
                  pl.BlockSpec((nb * L, W), lambda b, t: (b, 0)),
                  pl.BlockSpec((tt, L), lambda b, t: (t, 0)),
                  pl.BlockSpec((tt, L), lambda b, t: (t, 0)),
                  pl.BlockSpec((nb * tt, W), lambda b, t: (b * nt + t, ucol)),
                  pl.BlockSpec((nb * tt, W), lambda b, t: (b * nt + t, gcol)),
                  pl.BlockSpec((1, 1, W), lambda b, t: (f, 0, 0))],
        out_specs=pl.BlockSpec((nb * tt, W), lambda b, t: (b * nt + t, 0)),
        out_shape=jax.ShapeDtypeStruct((B * L, W), F32),
        scratch_shapes=[pltpu.VMEM((2, nb * L, W), BF16)],
        compiler_params=_cparams(("arbitrary", "arbitrary")),
        name="hyinv",
    )(yre, yim, fc, fst, u, gate, bias)


def _dft_consts(L):
    s = int(round(math.sqrt(L)))
    assert s * s == L
    k = jnp.arange(L, dtype=jnp.int32)[:, None]
    n1 = jnp.arange(s, dtype=jnp.int32)[None, :]
    a1 = ((k * (n1 * s)) % (2 * L)).astype(F32) * (math.pi / L)
    a2 = ((k * n1) % (2 * L)).astype(F32) * (math.pi / L)
    c1, s1, c2, s2 = jnp.cos(a1), jnp.sin(a1), jnp.cos(a2), jnp.sin(a2)
    fc = (c1[:, :, None] * c2[:, None, :] - s1[:, :, None] * s2[:, None, :]).reshape(L, L)
    sn = -(s1[:, :, None] * c2[:, None, :] + c1[:, :, None] * s2[:, None, :]).reshape(L, L)
    nyq = (1.0 - 2.0 * (jnp.arange(L) % 2)).astype(F32)
    fs = jnp.where(k == 0, nyq[None, :], sn)
    fst = jnp.where(k.T == 0, nyq[:, None], sn)
    return fc.astype(BF16), fs.astype(BF16), fst.astype(BF16)


def _hyena_consts(L):
    t = jnp.linspace(0.0, 1.0, L, dtype=F32)[:, None]
    ang = (2.0 * math.pi / L) * jnp.arange(L, dtype=F32)[:, None]
    bands = jnp.linspace(1e-4, HY_BANDS - 1, HY_BANDS, dtype=F32)[None, :]
    z = jnp.concatenate([t, jnp.cos(bands * ang), jnp.sin(bands * ang)], axis=-1)
    z = jnp.pad(z, ((0, 0), (0, 128 - HY_EMB)))
    deltas = jnp.abs(jnp.linspace(HY_MIN_DECAY, HY_MAX_DECAY, W, dtype=F32))
    dec = jnp.exp(-t * deltas)
    return z, dec


def _hyena(p, B, L, hc, lp):
    z, dec, fc, fs, fst = hc
    w1p = jnp.pad(lp['hy_w1'], ((0, 128 - HY_EMB), (0, 0)))
    row = lambda a: a.reshape(1, -1)
    h, nrm = _hyfilt(L, z, w1p, row(lp['hy_b1']), row(lp['hy_freq']), lp['hy_w2'], row(lp['hy_b2']),
                     lp['hy_w3'], dec)
    fre, fim = _hyspec(L, fc, fs, h, nrm)
    pc = _hypre(p, B, L, lp['hy_conv'].reshape(3, 3, W).transpose(1, 0, 2))
    bias = lp['hy_bias'].reshape(2, 1, W)
    yre, yim = _hyfwd(pc, 0, B, L, fc, fs, fre, fim, 0)
    zmid = _hyinv(yre, yim, B, L, fc, fst, pc, 0, pc, 1, bias, 0)
    yre, yim = _hyfwd(zmid, 0, B, L, fc, fs, fre, fim, 1)
    return _hyinv(yre, yim, B, L, fc, fst, zmid, 0, pc, 2, bias, 1)


def _mlstm_kernel(qf, kf, vf, gf, qb, kb, vb, gb, gbias, c0, n0, m0,
                  hf_o, hb_o, c_o, n_o, m_o, c_s, n_s, m_s, *, nc, nb):
    j = pl.program_id(1)
    T = ML_CHUNK

    @pl.when(j == 0)
    def _():
        c_s[...] = c0[...]
        n_s[...] = n0[...]
        m_s[...] = m0[...]

    lane = lax.broadcasted_iota(jnp.int32, (T, 128), 1)
    is_f = ((lane & 4) != 0) & (lane < 16)
    row = lax.broadcasted_iota(jnp.int32, (T, T), 0)
    col = lax.broadcasted_iota(jnp.int32, (T, T), 1)
    masks = (row >= col, row <= col)
    refs = ((qf, kf, vf, gf), (qb, kb, vb, gb))
    grp = [(d, i) for d in range(2) for i in range(nb)]
    Gs = []
    for d, i in grp:
        g = refs[d][3][i] + gbias[...]
        Gs.append(jnp.where(is_f, jnp.minimum(g, 0.0) - jnp.log1p(jnp.exp(-jnp.abs(g))), g))
    Bcs = [_sel_mm(masks[d].astype(BF16), G) for (d, i), G in zip(grp, Gs)]
    sr = lax.broadcasted_iota(jnp.int32, (128, NH * 128), 0)
    sc = lax.broadcasted_iota(jnp.int32, (128, NH * 128), 1) >> 7
    units = []
    for gi, (d, i) in enumerate(grp):
        G, Bc = Gs[gi], Bcs[gi]
        GT, BcT = G.T, Bc.T
        Bcol = _mm_sel(Bc, (sr == sc + (d * 8 + 4)).astype(BF16), 3)
        Icol = _mm_sel(G, (sr == sc + d * 8).astype(BF16), 3)
        q32 = refs[d][0][i] * (HD ** -0.5)
        q = q32.astype(BF16)
        k = refs[d][1][i]
        v = refs[d][2][i].astype(BF16)
        for h in range(NH):
            cf, ci = d * 8 + 4 + h, d * 8 + h
            sl = slice(h * HD, (h + 1) * HD)
            b_col = Bcol[:, h * 128:(h + 1) * 128]
            units.append(dict(
                d=d, i=i, u=d * NH + h, q=q[:, sl], q32=q32[:, sl], k=k[:, sl], v=v[:, sl],
                b_col=b_col, b_row=BcT[cf:cf + 1, :], i_col=Icol[:, h * 128:(h + 1) * 128], i_row=GT[ci:ci + 1, :],
                b_tot=b_col[T - 1:T, :] if d == 0 else b_col[0:1, :]))
    for un in units:
        un['m_prev'] = m_s[un['i'], un['u']:un['u'] + 1, :]
        un['C'] = c_s[un['i'], un['u']]
        un['n'] = n_s[un['i'], un['u']:un['u'] + 1, :]
    for un in units:
        un['qk'] = _mm_nt(un['q'], un['k'], False)
    for un in units:
        un['qC'] = _mm(un['q'], un['C'], False)
    for un in units:
        logd = jnp.where(masks[un['d']], un['b_col'] - un['b_row'] + un['i_row'], -jnp.inf)
        inter = un['b_col'] + un['m_prev']
        m_t = jnp.maximum(inter, jnp.max(logd, axis=-1, keepdims=True))
        un['s'] = un['qk'] * jnp.exp(logd - m_t)
        un['w_in'] = jnp.exp(inter - m_t)
        un['floor'] = jnp.exp(-m_t)
        logw = un['b_tot'] - un['b_col'] + un['i_col']
        m_new = jnp.maximum(un['b_tot'] + un['m_prev'], jnp.max(logw, axis=0, keepdims=True))
        un['kw'] = un['k'] * jnp.exp(logw - m_new)[:, 0:HD]
        un['decay'] = jnp.exp(un['b_tot'] + un['m_prev'] - m_new)[:, 0:HD]
        un['m_new'] = m_new
    for un in units:
        un['sv'] = _mm(un['s'], un['v'], False)
    for un in units:
        un['kv'] = _mm(un['kw'].T, un['v'], False)
    hs = {}
    for un in units:
        i, u = un['i'], un['u']
        num = un['sv'] + un['w_in'][:, 0:HD] * un['qC']
        den = jnp.sum(un['s'], axis=-1, keepdims=True) \
            + un['w_in'] * jnp.sum(un['q32'] * un['n'], axis=-1, keepdims=True)
        hs.setdefault((un['d'], i), []).append(num / jnp.maximum(jnp.abs(den), un['floor'])[:, 0:HD])
        c_s[i, u] = un['decay'] * un['C'] + un['kv']
        n_s[i, u:u + 1, :] = un['decay'] * un['n'] + jnp.sum(un['kw'], axis=0, keepdims=True)
        m_s[i, u:u + 1, :] = un['m_new']
    for (d, i), parts in hs.items():
        (hf_o, hb_o)[d][i] = jnp.concatenate(parts, axis=1)

    @pl.when(j == nc - 1)
    def _():
        c_o[...] = c_s[...]
        n_o[...] = n_s[...]
        m_o[...] = m_s[...]


def _mlstm_scan(p, B, L, gbias, c0, n0, m0):
    T = ML_CHUNK
    nc = L // T
    nb = min(B, 4)
    p3 = p.reshape(B, L, PC)
    fwd = lambda cb: (lambda b, j: (b, j, cb))
    bwd = lambda cb: (lambda b, j: (b, nc - 1 - j, cb))
    blk = lambda f, cb, wd=W: pl.BlockSpec((nb, T, wd), f(cb))
    cq, ck, cv, cg = C_ML // W, C_ML // W + 1, C_ML // W + 2, C_MLG // 128
    st = lambda shape: pl.BlockSpec((nb,) + shape, lambda b, j: (b,) + (0,) * len(shape))
    hf, hb, C, n, m = pl.pallas_call(
        functools.partial(_mlstm_kernel, nc=nc, nb=nb),
        grid=(B // nb, nc),
        in_specs=[blk(fwd, cq), blk(fwd, ck), blk(fwd, cv), blk(fwd, cg, 128),
                  blk(bwd, cq), blk(bwd, ck), blk(bwd, cv), blk(bwd, cg, 128),
                  pl.BlockSpec((1, 128), lambda b, j: (0, 0)),
                  st((8, HD, HD)), st((8, HD)), st((8, 128))],
        out_specs=[pl.BlockSpec((nb, T, W), lambda b, j: (b, j, 0)),
                   pl.BlockSpec((nb, T, W), lambda b, j: (b, nc - 1 - j, 0)),
                   st((8, HD, HD)), st((8, HD)), st((8, 128))],
        out_shape=[jax.ShapeDtypeStruct((B, L, W), F32), jax.ShapeDtypeStruct((B, L, W), F32),
                   jax.ShapeDtypeStruct((B, 8, HD, HD), F32), jax.ShapeDtypeStruct((B, 8, HD), F32),
                   jax.ShapeDtypeStruct((B, 8, 128), F32)],
        scratch_shapes=[pltpu.VMEM((nb, 8, HD, HD), F32), pltpu.VMEM((nb, 8, HD), F32),
                        pltpu.VMEM((nb, 8, 128), F32)],
        compiler_params=_cparams(("arbitrary", "arbitrary")),
        name="mlstm",
    )(p3, p3, p3, p3, p3, p3, p3, p3, gbias, c0, n0, m0)
    return hf.reshape(B * L, W), hb.reshape(B * L, W), C, n, m


def _shifted(x, prev8, next8, first, last):
    T = x.shape[0]
    rid = lax.broadcasted_iota(jnp.int32, x.shape, 0)
    pr = jnp.where(first, 0.0, prev8[7:8, :])
    nx = jnp.where(last, 0.0, next8[0:1, :])
    xp = jnp.where(rid == 0, pr, pltpu.roll(x, 1, axis=0))
    xn = jnp.where(rid == T - 1, nx, pltpu.roll(x, T - 1, axis=0))
    return xp, xn


def _rwprep_kernel(*refs, L, T):
    xs, prevs, nexts = refs[0:4], refs[4:8], refs[8:12]
    mu, w0, w2, a0, a2, g2, kkw, kaw, rkw = refs[12:21]
    r_o, v_o, kk_o, g_o, bon_o, lw0_o, lw1_o, kt0_o, kt1_o, ba0_o, ba1_o = refs[21:]
    i = pl.program_id(0)
    first = (i * T) % L == 0
    last = ((i + 1) * T) % L == 0
    pf = []
    for c in range(4):
        x = xs[c][...]
        xp, xn = _shifted(x, prevs[c][...], nexts[c][...], first, last)
        pf.append(x + mu[:, c * W:(c + 1) * W] * (0.5 * (xp + xn) - x))
    r, k, v, misc = pf
    lw, la, lg = misc[:, 0:64], misc[:, 64:128], misc[:, 128:256]
    ones = _head_ones(W)
    g_o[...] = _mm3(jax.nn.sigmoid(lg), g2[...])
    kq = k * kkw[...]
    kk = kq / jnp.maximum(jnp.sqrt(_mm_sel(kq * kq, ones)), 1e-12)
    tw = jnp.tanh(lw)
    r_o[...] = r
    v_o[...] = v
    kk_o[...] = kk
    bon = jnp.zeros_like(r)
    for d, (lw_o, kt_o, ba_o) in enumerate(((lw0_o, kt0_o, ba0_o), (lw1_o, kt1_o, ba1_o))):
        lw_o[...] = -RW_DECAY_SCALE * jax.nn.sigmoid(w0[d:d + 1, :] + _mm3(tw, w2[d]))
        a = jax.nn.sigmoid(a0[d:d + 1, :] + _mm3(la, a2[d]))
        kt = k * (1.0 + (a - 1.0) * kaw[...])
        kt_o[...] = kt
        ba_o[...] = kk * a
        bon = bon + _mm_sel(r * kt * rkw[...], ones) * v
    bon_o[...] = bon


def _rwprep(p, B, L, lp):
    n = B * L
    T = min(L, 512)
    nb8 = n // 8
    c0 = C_RW // W
    cur = lambda c: pl.BlockSpec((T, W), lambda i: (i, c0 + c))
    prv = lambda c: pl.BlockSpec((8, W), lambda i: (jnp.maximum(i * (T // 8) - 1, 0), c0 + c))
    nxt = lambda c: pl.BlockSpec((8, W), lambda i: (jnp.minimum((i + 1) * (T // 8), nb8 - 1), c0 + c))
    const = lambda shape: pl.BlockSpec(shape, lambda i: (0,) * len(shape))
    row = lambda a: a.reshape(1, -1)
    out = pl.BlockSpec((T, W), lambda i: (i, 0))
    return pl.pallas_call(
        functools.partial(_rwprep_kernel, L=L, T=T),
        grid=(n // T,),
        in_specs=[cur(c) for c in range(4)] + [prv(c) for c in range(4)] + [nxt(c) for c in range(4)]
        + [const((1, 4 * W)), const((2, W)), const((2, 64, W)), const((2, W)), const((2, 64, W)),
           const((128, W)), const((1, W)), const((1, W)), const((1, W))],
        out_specs=[out] * 11,
        out_shape=[jax.ShapeDtypeStruct((n, W), F32)] * 11,
        compiler_params=_cparams(("arbitrary",)),
        name="rwprep",
    )(*([p] * 12), row(lp['rw_mu']), lp['rw_w0'], lp['rw_w2'], lp['rw_a0'], lp['rw_a2'], lp['rw_g2'],
      row(lp['rw_kk']), row(lp['rw_ka']), row(lp['rw_rk']))


def _rwscan_kernel(rf, vf, kkf, lwf, ktf, baf, rb, vb, kkb, lwb, ktb, bab, s0,
                   yf_o, yb_o, s_o, s_s, *, nc, nb):
    j = pl.program_id(1)
    T = RW_CHUNK

    @pl.when(j == 0)
    def _():
        s_s[...] = s0[...]

    row = lax.broadcasted_iota(jnp.int32, (T, T), 0)
    col = lax.broadcasted_iota(jnp.int32, (T, T), 1)
    incl = (row >= col, row <= col)
    strict = (row > col, row < col)
    refs = ((rf, vf, kkf, lwf, ktf, baf), (rb, vb, kkb, lwb, ktb, bab))
    grp = [(d, i) for d in range(2) for i in range(nb)]
    lws = [refs[d][3][i] for d, i in grp]
    lcs = [_sel_mm(incl[d].astype(BF16), lw) for (d, i), lw in zip(grp, lws)]
    units = []
    for (d, i), lw, lc in zip(grp, lws, lcs):
        r_ref, v_ref, kk_ref, _, kt_ref, ba_ref = refs[d]
        gam = jnp.exp(lc)
        ginv = jnp.exp(-lc)
        at = -kk_ref[i] * jnp.exp(lc - lw)
        ar = jnp.concatenate([at, r_ref[i] * gam], axis=0).astype(BF16)
        bk = jnp.concatenate([ba_ref[i] * ginv, kt_ref[i] * ginv], axis=0).astype(BF16)
        v = v_ref[i]
        vb16 = v.astype(BF16)
        gtot = gam[T - 1:T, :] if d == 0 else gam[0:1, :]
        for h in range(NH):
            sl = slice(h * HD, (h + 1) * HD)
            units.append(dict(d=d, i=i, u=d * NH + h, ar=ar[:, sl], bk=bk[:, sl], v=v[:, sl], vb=vb16[:, sl],
                              gtot=gtot[:, sl]))
    for un in units:
        un['S'] = s_s[un['i'], un['u']]
    for un in units:
        un['P'] = lax.dot_general(un['ar'], un['bk'], _NT, preferred_element_type=F32)
    for un in units:
        un['AS'] = _mm_nt(un['ar'], un['S'], False)
    for un in units:
        P, d = un['P'], un['d']
        un['A_ab'] = jnp.where(strict[d], P[0:T, 0:T], 0.0)
        un['A_rb'] = jnp.where(incl[d], P[T:2 * T, 0:T], 0.0).astype(BF16)
        un['A_k'] = jnp.concatenate([jnp.where(strict[d], P[0:T, T:2 * T], 0.0),
                                     jnp.where(incl[d], P[T:2 * T, T:2 * T], 0.0)], axis=0).astype(BF16)
    for un in units:
        un['AV'] = jnp.dot(un['A_k'], un['vb'], preferred_element_type=F32)
    pair = ((row >> 1) == (col >> 1)) & (row != col)
    eye = (row == col).astype(F32)
    for un in units:
        un['X'] = eye + jnp.where(pair, un['A_ab'], 0.0)
    bs = 2
    while bs < T:
        sh = bs.bit_length() - 1
        pair = ((row >> (sh + 1)) == (col >> (sh + 1))) & ((row >> sh) != (col >> sh))
        for un in units:
            un['Xb'] = un['X'].astype(BF16)
            un['XN'] = jnp.dot(un['Xb'], jnp.where(pair, un['A_ab'], 0.0).astype(BF16), preferred_element_type=F32)
        for un in units:
            un['X'] = un['X'] + jnp.dot(un['XN'].astype(BF16), un['Xb'], preferred_element_type=F32)
        bs *= 2
    for un in units:
        un['U'] = _mm(un['X'], un['AS'][0:T] + un['AV'][0:T], False)
    for un in units:
        un['Y'] = un['AS'][T:2 * T] + un['AV'][T:2 * T] + jnp.dot(un['A_rb'], un['U'].astype(BF16),
                                                                  preferred_element_type=F32)
    for un in units:
        UV = jnp.concatenate([un['U'], un['v']], axis=0)
        s_s[un['i'], un['u']] = (un['S'] + _mm(UV.T, un['bk'], False)) * un['gtot']
    ys = {}
    for un in units:
        ys.setdefault((un['d'], un['i']), []).append(un['Y'])
    for (d, i), parts in ys.items():
        (yf_o, yb_o)[d][i] = jnp.concatenate(parts, axis=1)

    @pl.when(j == nc - 1)
    def _():
        s_o[...] = s_s[...]


def _rwscan(prep, B, L, s0):
    T = RW_CHUNK
    nc = L // T
    nb = min(B, 4)
    r, v, kk, lw0, lw1, kt0, kt1, ba0, ba1 = [a.reshape(B, L, W) for a in prep]
    fwd = pl.BlockSpec((nb, T, W), lambda b, j: (b, j, 0))
    bwd = pl.BlockSpec((nb, T, W), lambda b, j: (b, nc - 1 - j, 0))
    st = pl.BlockSpec((nb, 8, HD, HD), lambda b, j: (b, 0, 0, 0))
    yf, yb, S = pl.pallas_call(
        functools.partial(_rwscan_kernel, nc=nc, nb=nb),
        grid=(B // nb, nc),
        in_specs=[fwd] * 6 + [bwd] * 6 + [st],
        out_specs=[fwd, bwd, st],
        out_shape=[jax.ShapeDtypeStruct((B, L, W), F32), jax.ShapeDtypeStruct((B, L, W), F32),
                   jax.ShapeDtypeStruct((B, 8, HD, HD), F32)],
        scratch_shapes=[pltpu.VMEM((nb, 8, HD, HD), F32)],
        compiler_params=_cparams(("arbitrary", "arbitrary")),
        name="rwscan",
    )(r, v, kk, lw0, kt0, ba0, r, v, kk, lw1, kt1, ba1, s0)
    return yf.reshape(B * L, W), yb.reshape(B * L, W), S


def _rms(x, gain):
    ms = _mm_sel(x * x, _head_ones(x.shape[1])) * (1.0 / HD)
    return x * lax.rsqrt(ms + 1e-6) * gain


def _rope(x, cos, sin):
    wd = x.shape[1]
    lane = lax.broadcasted_iota(jnp.int32, x.shape, 1)
    partner = jnp.where((lane & 16) == 0, pltpu.roll(x, wd - 16, axis=1), pltpu.roll(x, 16, axis=1))
    return x * cos + partner * sin


def _attn_kernel(*refs, rope, cached, tq, kchunk):
    if rope:
        q_ref, k_ref, v_ref, qn, kn, cosq, sinq, cosk, sink = refs[:9]
        rest = refs[9:]
    else:
        q_ref, k_ref, v_ref, qn, kn = refs[:5]
        rest = refs[5:]
    if cached:
        kc_ref, vc_ref = rest[:2]
        rest = rest[2:]
    o_ref, kn_o, k_s, v_s = rest
    t = pl.program_id(1)
    L = k_ref.shape[0]
    Lk = k_s.shape[0]

    def with_ones(v):
        ones = jnp.ones((v.shape[0], HD), BF16)
        return jnp.concatenate([v[:, 0:HD].astype(BF16), ones, v[:, HD:2 * HD].astype(BF16), ones], axis=1)

    @pl.when(t == 0)
    def _():
        k = _rms(k_ref[...], kn[...])
        if rope:
            k = _rope(k, cosk[...], sink[...])
        kn_o[...] = k
        k_s[0:L, :] = k.astype(BF16)
        v_s[0:L, :] = with_ones(v_ref[...])
        if cached:
            k_s[L:Lk, :] = kc_ref[0].astype(BF16)
            v_s[L:Lk, :] = with_ones(vc_ref[0])

    q = _rms(q_ref[...], qn[...])
    if rope:
        q = _rope(q, cosq[...], sinq[...])
    q = (q * (HD ** -0.5 * math.log2(math.e))).astype(BF16)
    for jkv in range(2):
        sl = slice(jkv * HD, (jkv + 1) * HD)
        q2 = jnp.concatenate([q[:, (2 * jkv) * HD:(2 * jkv + 1) * HD],
                              q[:, (2 * jkv + 1) * HD:(2 * jkv + 2) * HD]], axis=0)
        m = jnp.full((2 * tq, 1), -jnp.inf, F32)
        acc = jnp.zeros((2 * tq, 2 * HD), F32)
        for c in range(0, Lk, kchunk):
            n = min(kchunk, Lk - c)
            kc = k_s[c:c + n, sl]
            vc = v_s[c:c + n, 2 * jkv * HD:2 * (jkv + 1) * HD]
            s = _mm_nt(q2, kc, hi=False)
            m_new = jnp.maximum(m, jnp.max(s, axis=-1, keepdims=True))
            pr = jnp.exp2(s - m_new)
            acc = jnp.exp2(m - m_new) * acc + jnp.dot(pr.astype(BF16), vc, preferred_element_type=F32)
            m = m_new
        o = acc[:, 0:HD] / acc[:, HD:2 * HD]
        o_ref[:, (2 * jkv) * HD:(2 * jkv + 1) * HD] = o[0:tq]
        o_ref[:, (2 * jkv + 1) * HD:(2 * jkv + 2) * HD] = o[tq:2 * tq]


def _attn(p, B, L, qn, kn, rope_tabs=None, cache=None):
    tq = min(L, 256)
    nt = L // tq
    kchunk = min(L, 1024)
    rope, cached = rope_tabs is not None, cache is not None
    cq, ck, cv = C_AT // W, (C_AT + W) // 128, (C_AT + W) // 128 + 1
    const = lambda shape: pl.BlockSpec(shape, lambda b, t: (0, 0))
    in_specs = [pl.BlockSpec((tq, W), lambda b, t: (b * nt + t, cq)),
                pl.BlockSpec((L, 128), lambda b, t: (b, ck)),
                pl.BlockSpec((L, 128), lambda b, t: (b, cv)),
                const((1, W)), const((1, 128))]
    args = [p, p, p, qn, kn]
    Lk = L
    if rope:
        cos, sin = rope_tabs
        in_specs += [pl.BlockSpec((tq, W), lambda b, t: (t, 0)), pl.BlockSpec((tq, W), lambda b, t: (t, 0)),
                     const((L, 128)), const((L, 128))]
        args += [cos, sin, cos, sin]
    if cached:
        kc, vc = cache
        P = kc.shape[1]
        Lk = L + P
        in_specs += [pl.BlockSpec((1, P, 128), lambda b, t: (b, 0, 0))] * 2
        args += [kc, vc]
    return pl.pallas_call(
        functools.partial(_attn_kernel, rope=rope, cached=cached, tq=tq, kchunk=kchunk),
        grid=(B, nt),
        in_specs=in_specs,
        out_specs=[pl.BlockSpec((tq, W), lambda b, t: (b * nt + t, 0)),
                   pl.BlockSpec((L, 128), lambda b, t: (b, 0))],
        out_shape=[jax.ShapeDtypeStruct((B * L, W), F32), jax.ShapeDtypeStruct((B * L, 128), F32)],
        scratch_shapes=[pltpu.VMEM((Lk, 128), BF16), pltpu.VMEM((Lk, 256), BF16)],
        compiler_params=_cparams(("arbitrary", "arbitrary")),
        name="attn",
    )(*args)


def _rope_tables(L):
    t = jnp.arange(L)
    pos_r = (t // GRID_W).astype(F32)[:, None]
    pos_c = (t % GRID_W).astype(F32)[:, None]
    nf = HD // 4
    inv = ROPE_THETA ** (-jnp.arange(nf, dtype=F32) / nf)[None, :]
    ar, ac = pos_r * inv, pos_c * inv
    cos = jnp.concatenate([jnp.cos(ar), jnp.cos(ar), jnp.cos(ac), jnp.cos(ac)], axis=1)
    sin = jnp.concatenate([-jnp.sin(ar), jnp.sin(ar), -jnp.sin(ac), jnp.sin(ac)], axis=1)
    return jnp.tile(cos, (1, NH)), jnp.tile(sin, (1, NH))


def _layer(x, B, L, mod8, base, rows_per_cond, lp, consts, ctx):
    p = _inproj(x, mod8, lp['w_in_p'], rows_per_cond, base)
    y_hy = _hyena(p, B, L, consts['hy'], lp)

    if ctx is None:
        c0 = jnp.zeros((B, 8, HD, HD), F32)
        n0 = jnp.zeros((B, 8, HD), F32)
        m0 = jnp.zeros((B, 8, 128), F32)
        s0 = jnp.zeros((B, 8, HD, HD), F32)
        cache = None
    else:
        kc, vc, c0, n0, m0, s0 = ctx
        P = kc.shape[1]
        cache = (kc.reshape(B, P, 128), vc.reshape(B, P, 128))
        c0 = c0.reshape(B, 8, HD, HD)
        n0 = n0.reshape(B, 8, HD)
        m0 = jnp.broadcast_to(m0.reshape(B, 8, 1), (B, 8, 128))
        s0 = s0.reshape(B, 8, HD, HD)

    gbias = jnp.pad(lp['ml_gate_b'], (0, 128 - 4 * NH)).reshape(1, 128)
    hf, hb, C, n, m = _mlstm_scan(p, B, L, gbias, c0, n0, m0)

    r, v, kk, g, bon, lw0, lw1, kt0, kt1, ba0, ba1 = _rwprep(p, B, L, lp)
    yf, yb, S = _rwscan((r, v, kk, lw0, lw1, kt0, kt1, ba0, ba1), B, L, s0)

    qn = jnp.tile(lp['at_qn'], NH).reshape(1, W)
    kn = jnp.tile(lp['at_kn'], 2).reshape(1, 128)
    y_at, k_new = _attn(p, B, L, qn, kn, consts.get('rope'), cache)
    v_new = p[:, C_AT + W + 128:C_AT + W + 256]

    row = lambda a: a.reshape(1, D)
    vec = lambda a: a.reshape(1, W)
    ys = (y_hy, hf, hb, p, vec(lp['ml_norm_g']), yf, yb, bon, g, vec(lp['rw_ln_g']), vec(lp['rw_ln_b']), y_at)
    x = _outmlp(ys, x, mod8, lp['w_out_b'], row(lp['ln1_g']), row(lp['ln1_b']),
                lp['mlp_w1_b'], lp['mlp_w2_b'], row(lp['ln2_g']), row(lp['ln2_b']), rows_per_cond, base)
    state = (k_new.reshape(B, L, 2, HD), v_new.reshape(B, L, 2, HD), C.reshape(B, 2, NH, HD, HD),
             n.reshape(B, 2, NH, HD), m[:, :, 0].reshape(B, 2, NH), S.reshape(B, 2, NH, HD, HD))
    return x, state


def _permute_w_in(w_in):
    hy = w_in[:, :, 0:768]
    ml = w_in[:, :, 768:1792]
    mlg = w_in[:, :, 1792:1808]
    rw = w_in[:, :, 1808:2832]
    at = w_in[:, :, 2832:3344]
    mlg = jnp.pad(mlg, ((0, 0), (0, 0), (0, 128 - 16)))
    return jnp.concatenate([hy, ml, rw, at, mlg], axis=-1).astype(BF16)


def kernel(x_prompt, x_sample, cache_attn_k, cache_attn_v, state_mlstm_C, state_mlstm_n, state_mlstm_m, state_rwkv_S, c, c_ctx, w_mod, b_mod, w_in, hy_conv, hy_w1, hy_b1, hy_freq, hy_w2, hy_b2, hy_w3, hy_bias, ml_gate_b, ml_norm_g, rw_mu, rw_w0, rw_w2, rw_a0, rw_a2, rw_g2, rw_kk, rw_ka, rw_rk, rw_ln_g, rw_ln_b, at_qn, at_kn, w_out, ln1_g, ln1_b, mlp_w1, mlp_w2, ln2_g, ln2_b):
    Bp, Lp, _ = x_prompt.shape
    Bs, Ls, _ = x_sample.shape
    params = {
        'w_in_p': _permute_w_in(w_in), 'hy_conv': hy_conv, 'hy_w1': hy_w1, 'hy_b1': hy_b1, 'hy_freq': hy_freq,
        'hy_w2': hy_w2, 'hy_b2': hy_b2, 'hy_w3': hy_w3, 'hy_bias': hy_bias,
        'ml_gate_b': ml_gate_b, 'ml_norm_g': ml_norm_g,
        'rw_mu': rw_mu, 'rw_w0': rw_w0, 'rw_w2': rw_w2, 'rw_a0': rw_a0, 'rw_a2': rw_a2, 'rw_g2': rw_g2,
        'rw_kk': rw_kk, 'rw_ka': rw_ka, 'rw_rk': rw_rk, 'rw_ln_g': rw_ln_g, 'rw_ln_b': rw_ln_b,
        'at_qn': at_qn, 'at_kn': at_kn, 'w_out_b': w_out.astype(BF16), 'ln1_g': ln1_g, 'ln1_b': ln1_b,
        'mlp_w1_b': mlp_w1.astype(BF16), 'mlp_w2_b': mlp_w2.astype(BF16), 'ln2_g': ln2_g, 'ln2_b': ln2_b,
    }
    cvec = jnp.concatenate([c_ctx[None, :], c, jnp.zeros((8 - 1 - Bs, D), F32)], axis=0)
    mod = _mod_all(cvec, w_mod, b_mod).reshape(DEPTH, 8, 1, 6 * D)

    consts_p = {'hy': _hyena_consts(Lp) + _dft_consts(Lp)}
    consts_s = {'hy': _hyena_consts(Ls) + _dft_consts(Ls), 'rope': _rope_tables(Ls)}

    x = x_prompt.reshape(Bp * Lp, D)
    states = []
    for l in range(DEPTH):
        lp = {name: arr[l] for name, arr in params.items()}
        x, st = _layer(x, Bp, Lp, mod[l], 0, Bp * Lp, lp, consts_p, None)
        states.append(st)
    y_prompt = x.reshape(Bp, Lp, D)
    outs = [jnp.stack([st[i] for st in states], axis=1) for i in range(6)]

    x = x_sample.reshape(Bs * Ls, D)
    for l in range(DEPTH):
        lp = {name: arr[l] for name, arr in params.items()}
        ctx = (cache_attn_k[:, l], cache_attn_v[:, l], state_mlstm_C[:, l], state_mlstm_n[:, l],
               state_mlstm_m[:, l], state_rwkv_S[:, l])
        x, _ = _layer(x, Bs, Ls, mod[l], 1, Ls, lp, consts_s, ctx)
    y_sample = x.reshape(Bs, Ls, D)
    return (y_prompt, y_sample) + tuple(outs)
```

```python
import functools
import math

import jax
import jax.numpy as jnp
from jax import lax
from jax.experimental import pallas as pl
from jax.experimental.pallas import tpu as pltpu

F32 = jnp.float32
BF16 = jnp.bfloat16
HI = lax.Precision.HIGHEST

D = 1024
DEPTH = 4
W = 256
HD = 64
NH = 4
GRID_W = 64
D_FF = 4 * D
ALPHA = (2.0 * DEPTH) ** 0.25
HY_BANDS = 16
HY_EMB = 2 * HY_BANDS + 1
HY_FFN = 64
HY_MAX_DECAY = math.log(1e-2) / 0.3
HY_MIN_DECAY = math.log(1e-2) / 1.5
ML_CHUNK = 128
RW_CHUNK = 64
RW_DECAY_SCALE = 0.606531
RW_GN_EPS = 64e-5
ROPE_THETA = 10000.0

PC = 3456
C_HY = 0
C_ML = 768
C_RW = 1792
C_RWX = 2560
C_AT = 2816
C_MLG = 3328

VMEM_LIMIT = 56 * 1024 * 1024


def _cparams(sem):
    return pltpu.CompilerParams(dimension_semantics=sem, vmem_limit_bytes=VMEM_LIMIT)


_NN = (((1,), (0,)), ((), ()))
_NT = (((1,), (1,)), ((), ()))


def _dot(a, b, dn, hi):
    if hi:
        return lax.dot_general(a, b, dn, preferred_element_type=F32, precision=HI)
    return lax.dot_general(a.astype(BF16), b.astype(BF16), dn, preferred_element_type=F32)


def _mm(a, b, hi=True):
    return _dot(a, b, _NN, hi)


def _mm_nt(a, b, hi=True):
    return _dot(a, b, _NT, hi)


def _pieces(x, n):
    out = []
    for _ in range(n - 1):
        p = x.astype(BF16)
        out.append(p)
        x = x - p.astype(F32)
    out.append(x.astype(BF16))
    return out


def _mm3(a, b):
    ah, al = _pieces(a, 2)
    bh, bl = _pieces(b, 2)
    d = lambda x, y: jnp.dot(x, y, preferred_element_type=F32)
    return d(ah, bh) + (d(ah, bl) + d(al, bh))


def _sel_mm(sel, x, n=3):
    acc = None
    for p in _pieces(x, n):
        t = jnp.dot(sel, p, preferred_element_type=F32)
        acc = t if acc is None else acc + t
    return acc


def _mm_sel(x, sel, n=2):
    acc = None
    for p in _pieces(x, n):
        t = jnp.dot(p, sel, preferred_element_type=F32)
        acc = t if acc is None else acc + t
    return acc


def _head_ones(width):
    r = lax.broadcasted_iota(jnp.int32, (width, width), 0) >> 6
    c = lax.broadcasted_iota(jnp.int32, (width, width), 1) >> 6
    return (r == c).astype(BF16)


def _layernorm(x, g, b):
    mu = jnp.mean(x, -1, keepdims=True)
    xc = x - mu
    var = jnp.mean(xc * xc, -1, keepdims=True)
    return xc * lax.rsqrt(var + 1e-5) * g + b


def _head_norm(x, eps):
    ones = _head_ones(x.shape[1])
    mu = _mm_sel(x, ones) * (1.0 / HD)
    xc = x - mu
    var = _mm_sel(xc * xc, ones) * (1.0 / HD)
    return xc * lax.rsqrt(var + eps)


def _mod_kernel(c_ref, w_ref, b_ref, o_ref):
    c = c_ref[...]
    s = c * jax.nn.sigmoid(c)
    o_ref[0] = _mm(s, w_ref[0]) + b_ref[0]


def _mod_all(cvec, w_mod, b_mod):
    tn = 1536
    return pl.pallas_call(
        _mod_kernel,
        grid=(DEPTH, 6 * D // tn),
        in_specs=[pl.BlockSpec((8, D), lambda l, n: (0, 0)),
                  pl.BlockSpec((1, D, tn), lambda l, n: (l, 0, n)),
                  pl.BlockSpec((1, 1, tn), lambda l, n: (l, 0, n))],
        out_specs=pl.BlockSpec((1, 8, tn), lambda l, n: (l, 0, n)),
        out_shape=jax.ShapeDtypeStruct((DEPTH, 8, 6 * D), F32),
        compiler_params=_cparams(("arbitrary", "arbitrary")),
        name="mod",
    )(cvec, w_mod, b_mod.reshape(DEPTH, 1, 6 * D))


def _mod_row_map(rows_per_cond, tm, base):
    return lambda i: (base + (i * tm) // rows_per_cond, 0, 0)


def _inproj_kernel(x_ref, mod_ref, w_ref, o_ref):
    m = mod_ref[0]
    h = x_ref[...] * (1.0 + m[:, D:2 * D]) + m[:, 0:D]
    o_ref[...] = jnp.dot(h.astype(BF16), w_ref[...], preferred_element_type=F32)


def _inproj(x, mod8, w_in_p, rows_per_cond, base):
    n = x.shape[0]
    tm, tn = min(n, 2048), 1152
    rmap = _mod_row_map(rows_per_cond, tm, base)
    return pl.pallas_call(
        _inproj_kernel,
        grid=(PC // tn, n // tm),
        in_specs=[pl.BlockSpec((tm, D), lambda c, i: (i, 0)),
                  pl.BlockSpec((1, 1, 6 * D), lambda c, i: rmap(i)),
                  pl.BlockSpec((D, tn), lambda c, i: (0, c))],
        out_specs=pl.BlockSpec((tm, tn), lambda c, i: (i, c)),
        out_shape=jax.ShapeDtypeStruct((n, PC), F32),
        compiler_params=_cparams(("arbitrary", "arbitrary")),
        name="inproj",
    )(x, mod8, w_in_p)


def _outmlp_kernel(yh, hf, hb, og, mlg, yf, yb, bon, rg, lng, lnb, ya, x_ref, mod_ref,
                   wo, g1, b1, w1, w2, g2, b2, o_ref):
    m = mod_ref[0]
    y_ml = _head_norm(hf[...] + hb[...], 1e-6) * mlg[...] * jax.nn.sigmoid(og[...])
    y_rw = (_head_norm(yf[...] + yb[...], RW_GN_EPS) * lng[...] + lnb[...] + bon[...]) * rg[...]
    y = jnp.concatenate([yh[...], y_ml, y_rw, ya[...]], axis=-1).astype(BF16)
    mix = jnp.dot(y, wo[...], preferred_element_type=F32)
    x1 = _layernorm(ALPHA * x_ref[...] + m[:, 2 * D:3 * D] * mix, g1[...], b1[...])
    h = (x1 * (1.0 + m[:, 4 * D:5 * D]) + m[:, 3 * D:4 * D]).astype(BF16)
    acc = jnp.zeros(x1.shape, F32)
    for c in range(D_FF // D):
        a = jnp.dot(h, w1[:, c * D:(c + 1) * D], preferred_element_type=F32)
        a = jnp.square(jnp.maximum(a, 0.0)).astype(BF16)
        acc = acc + jnp.dot(a, w2[c * D:(c + 1) * D, :], preferred_element_type=F32)
    o_ref[...] = _layernorm(ALPHA * x1 + m[:, 5 * D:6 * D] * acc, g2[...], b2[...])


def _outmlp(ys, x, mod8, wo, g1, b1, w1, w2, g2, b2, rows_per_cond, base):
    n = x.shape[0]
    tm = 512
    rmap = _mod_row_map(rows_per_cond, tm, base)
    const = lambda shape: pl.BlockSpec(shape, lambda i: (0, 0), pipeline_mode=pl.Buffered(1))
    row = lambda wd: pl.BlockSpec((tm, wd), lambda i: (i, 0))
    vec = pl.BlockSpec((1, W), lambda i: (0, 0))
    return pl.pallas_call(
        _outmlp_kernel,
        grid=(n // tm,),
        in_specs=[row(W), row(W), row(W), pl.BlockSpec((tm, W), lambda i: (i, C_ML // W + 3)), vec,
                  row(W), row(W), row(W), row(W), vec, vec, row(W), row(D),
                  pl.BlockSpec((1, 1, 6 * D), lambda i: rmap(i)),
                  const((D, D)), const((1, D)), const((1, D)),
                  const((D, D_FF)), const((D_FF, D)), const((1, D)), const((1, D))],
        out_specs=row(D),
        out_shape=jax.ShapeDtypeStruct((n, D), F32),
        compiler_params=_cparams(("arbitrary",)),
        name="outmlp",
    )(*ys, x, mod8, wo, g1, b1, w1, w2, g2, b2)


def _hyfilt_kernel(z_ref, w1, b1, fr, w2, b2, w3, dec_ref, h_ref, nrm_ref):
    i = pl.program_id(0)
    f = fr[...]
    hid = jnp.sin(f * (_mm(z_ref[...], w1[...]) + b1[...]))
    hid = jnp.sin(f * (_mm(hid, w2[...]) + b2[...]))
    h = _mm(hid, w3[...])
    dec = dec_ref[...]
    h = h * jnp.concatenate([dec, dec, dec, dec], axis=1)
    rid = lax.broadcasted_iota(jnp.int32, h.shape, 0)
    cid = lax.broadcasted_iota(jnp.int32, h.shape, 1)
    anti = ((cid >> 8) & 1) == 1
    h = jnp.where(anti & (rid == 0) & (i == 0), 0.0, h)
    h_ref[...] = h.astype(BF16)
    part = jnp.broadcast_to(jnp.sum(jnp.abs(h), axis=0, keepdims=True), (8, 4 * W))

    @pl.when(i == 0)
    def _():
        nrm_ref[...] = part

    @pl.when(i > 0)
    def _():
        nrm_ref[...] = nrm_ref[...] + part


def _hyfilt(L, z, w1p, b1, fr, w2, b2, w3, dec):
    tl = min(L, 512)
    const = lambda shape: pl.BlockSpec(shape, lambda i: (0, 0))
    return pl.pallas_call(
        _hyfilt_kernel,
        grid=(L // tl,),
        in_specs=[pl.BlockSpec((tl, 128), lambda i: (i, 0)),
                  const((128, HY_FFN)), const((1, HY_FFN)), const((1, HY_FFN)),
                  const((HY_FFN, HY_FFN)), const((1, HY_FFN)), const((HY_FFN, 4 * W)),
                  pl.BlockSpec((tl, W), lambda i: (i, 0))],
        out_specs=[pl.BlockSpec((tl, 4 * W), lambda i: (i, 0)), const((8, 4 * W))],
        out_shape=[jax.ShapeDtypeStruct((L, 4 * W), BF16), jax.ShapeDtypeStruct((8, 4 * W), F32)],
        compiler_params=_cparams(("arbitrary",)),
        name="hyfilt",
    )(z, w1p, b1, fr, w2, b2, w3, dec)


def _hyspec_kernel(fc, fs, h_ref, nrm_ref, re_ref, im_ref, *, L):
    k = pl.program_id(0)
    h = h_ref[...]
    re = jnp.dot(fc[...], h, preferred_element_type=F32)
    im = jnp.dot(fs[...], h, preferred_element_type=F32)
    tk = re.shape[0]
    rid = lax.broadcasted_iota(jnp.int32, (tk, W), 0) + k * tk
    packed = rid == 0
    scale = jnp.where(packed, 1.0, 2.0) * (1.0 / (2 * L))
    nrm = nrm_ref[0:1, :]
    res, ims = [], []
    for f in range(2):
        c0, c1 = 2 * f * W, (2 * f + 1) * W
        inv = scale / (nrm[:, c0:c0 + W] + nrm[:, c1:c1 + W])
        res.append((re[:, c0:c0 + W] + re[:, c1:c1 + W]) * inv)
        ims.append(jnp.where(packed, im[:, c0:c0 + W] + im[:, c1:c1 + W],
                             im[:, c0:c0 + W] - im[:, c1:c1 + W]) * inv)
    re_ref[...] = jnp.concatenate(res, axis=1)
    im_ref[...] = jnp.concatenate(ims, axis=1)


def _hyspec(L, fc, fs, h, nrm):
    tk = min(L, 512)
    return pl.pallas_call(
        functools.partial(_hyspec_kernel, L=L),
        grid=(L // tk,),
        in_specs=[pl.BlockSpec((tk, L), lambda k: (k, 0)),
                  pl.BlockSpec((tk, L), lambda k: (k, 0)),
                  pl.BlockSpec((L, 4 * W), lambda k: (0, 0), pipeline_mode=pl.Buffered(1)),
                  pl.BlockSpec((8, 4 * W), lambda k: (0, 0))],
        out_specs=[pl.BlockSpec((tk, 2 * W), lambda k: (k, 0))] * 2,
        out_shape=[jax.ShapeDtypeStruct((L, 2 * W), F32)] * 2,
        compiler_params=_cparams(("arbitrary",)),
        name="hyspec",
    )(fc, fs, h, nrm)


def _seqs_per_step(B, L):
    return max(1, min(B, 2048 // L))


def _hypre_kernel(x_ref, w_ref, o_ref, *, L):
    x = x_ref[...]
    R = x.shape[0]
    w = w_ref[0]
    pos = lax.broadcasted_iota(jnp.int32, x.shape, 0) & (L - 1)
    xp = jnp.where(pos == 0, 0.0, pltpu.roll(x, 1, axis=0))
    xn = jnp.where(pos == L - 1, 0.0, pltpu.roll(x, R - 1, axis=0))
    o_ref[...] = xp * w[0:1, :] + x * w[1:2, :] + xn * w[2:3, :]


def _hypre(p, B, L, conv_w3):
    assert L & (L - 1) == 0
    nb = _seqs_per_step(B, L)
    return pl.pallas_call(
        functools.partial(_hypre_kernel, L=L),
        grid=(B // nb, 3),
        in_specs=[pl.BlockSpec((nb * L, W), lambda b, c: (b, c)),
                  pl.BlockSpec((1, 3, W), lambda b, c: (c, 0, 0))],
        out_specs=pl.BlockSpec((nb * L, W), lambda b, c: (b, c)),
        out_shape=jax.ShapeDtypeStruct((B * L, 3 * W), F32),
        compiler_params=_cparams(("arbitrary", "arbitrary")),
        name="hypre",
    )(p, conv_w3)


def _hyfwd_kernel(u_ref, fc, fs, fre_ref, fim_ref, yre_ref, yim_ref, ub, *, L, nb):
    k = pl.program_id(1)

    @pl.when(k == 0)
    def _():
        ub[...] = u_ref[...].astype(BF16)

    fre, fim = fre_ref[...], fim_ref[...]
    tk = fre.shape[0]
    packed = (lax.broadcasted_iota(jnp.int32, fre.shape, 0) + k * tk) == 0
    for s in range(nb):
        u = ub[s * L:(s + 1) * L, :]
        re = jnp.dot(fc[...], u, preferred_element_type=F32)
        im = jnp.dot(fs[...], u, preferred_element_type=F32)
        yre_ref[s * tk:(s + 1) * tk, :] = jnp.where(packed, re * fre, re * fre - im * fim)
        yim_ref[s * tk:(s + 1) * tk, :] = jnp.where(packed, im * fim, re * fim + im * fre)


def _hyfwd(u, ucol, B, L, fc, fs, fre, fim, f):
    tk = min(L, 1024)
    nb = _seqs_per_step(B, L) if tk == L else 1
    nk = L // tk
    return pl.pallas_call(
        functools.partial(_hyfwd_kernel, L=L, nb=nb),
        grid=(B // nb, nk),
        in_specs=[pl.BlockSpec((nb * L, W), lambda b, k: (b, ucol)),
                  pl.BlockSpec((tk, L), lambda b, k: (k, 0)),
                  pl.BlockSpec((tk, L), lambda b, k: (k, 0)),
                  pl.BlockSpec((tk, W), lambda b, k: (k, f)),
                  pl.BlockSpec((tk, W), lambda b, k: (k, f))],
        out_specs=[pl.BlockSpec((nb * tk, W), lambda b, k: (b * nk + k, 0))] * 2,
        out_shape=[jax.ShapeDtypeStruct((B * L, W), F32)] * 2,
        scratch_shapes=[pltpu.VMEM((nb * L, W), BF16)],
        compiler_params=_cparams(("arbitrary", "arbitrary")),
        name="hyfwd",
    )(u, fc, fs, fre, fim)


def _hyinv_kernel(yre_ref, yim_ref, fc, fst, u_ref, g_ref, bias_ref, o_ref, yb, *, L, nb):
    t = pl.program_id(1)

    @pl.when(t == 0)
    def _():
        yb[0] = yre_ref[...].astype(BF16)
        yb[1] = yim_ref[...].astype(BF16)

    tt = fc.shape[0]
    for s in range(nb):
        y = jnp.dot(fc[...], yb[0, s * L:(s + 1) * L, :], preferred_element_type=F32) \
            + jnp.dot(fst[...], yb[1, s * L:(s + 1) * L, :], preferred_element_type=F32)
        rows = slice(s * tt, (s + 1) * tt)
        o_ref[rows, :] = g_ref[rows, :] * (y + u_ref[rows, :] * bias_ref[0])


def _hyinv(yre, yim, B, L, fc, fst, u, ucol, gate, gcol, bias, f):
    tt = min(L, 512)
    nt = L // tt
    nb = _seqs_per_step(B, L) if tt == L else 1
    return pl.pallas_call(
        functools.partial(_hyinv_kernel, L=L, nb=nb),
        grid=(B // nb, nt),
        in_specs=[pl.BlockSpec((nb * L, W), lambda b, t: (b, 0)),
                  pl.BlockSpec((nb * L, W), lambda b, t: (b, 0)),
                  pl.BlockSpec((tt, L), lambda b, t: (t, 0)),
                  pl.BlockSpec((tt, L), lambda b, t: (t, 0)),
                  pl.BlockSpec((nb * tt, W), lambda b, t: (b * nt + t, ucol)),
                  pl.BlockSpec((nb * tt, W), lambda b, t: (b * nt + t, gcol)),
                  pl.BlockSpec((1, 1, W), lambda b, t: (f, 0, 0))],
        out_specs=pl.BlockSpec((nb * tt, W), lambda b, t: (b * nt + t, 0)),
        out_shape=jax.ShapeDtypeStruct((B * L, W), F32),
        scratch_shapes=[pltpu.VMEM((2, nb * L, W), BF16)],
        compiler_params=_cparams(("arbitrary", "arbitrary")),
        name="hyinv",
    )(yre, yim, fc, fst, u, gate, bias)


def _dft_consts(L):
    s = int(round(math.sqrt(L)))
    assert s * s == L
    k = jnp.arange(L, dtype=jnp.int32)[:, None]
    n1 = jnp.arange(s, dtype=jnp.int32)[None, :]
    a1 = ((k * (n1 * s)) % (2 * L)).astype(F32) * (math.pi / L)
    a2 = ((k * n1) % (2 * L)).astype(F32) * (math.pi / L)
    c1, s1, c2, s2 = jnp.cos(a1), jnp.sin(a1), jnp.cos(a2), jnp.sin(a2)
    fc = (c1[:, :, None] * c2[:, None, :] - s1[:, :, None] * s2[:, None, :]).reshape(L, L)
    sn = -(s1[:, :, None] * c2[:, None, :] + c1[:, :, None] * s2[:, None, :]).reshape(L, L)
    nyq = (1.0 - 2.0 * (jnp.arange(L) % 2)).astype(F32)
    fs = jnp.where(k == 0, nyq[None, :], sn)
    fst = jnp.where(k.T == 0, nyq[:, None], sn)
    return fc.astype(BF16), fs.astype(BF16), fst.astype(BF16)


def _hyena_consts(L):
    t = jnp.linspace(0.0, 1.0, L, dtype=F32)[:, None]
    ang = (2.0 * math.pi / L) * jnp.arange(L, dtype=F32)[:, None]
    bands = jnp.linspace(1e-4, HY_BANDS - 1, HY_BANDS, dtype=F32)[None, :]
    z = jnp.concatenate([t, jnp.cos(bands * ang), jnp.sin(bands * ang)], axis=-1)
    z = jnp.pad(z, ((0, 0), (0, 128 - HY_EMB)))
    deltas = jnp.abs(jnp.linspace(HY_MIN_DECAY, HY_MAX_DECAY, W, dtype=F32))
    dec = jnp.exp(-t * deltas)
    return z, dec


def _hyena(p, B, L, hc, lp):
    z, dec, fc, fs, fst = hc
    w1p = jnp.pad(lp['hy_w1'], ((0, 128 - HY_EMB), (0, 0)))
    row = lambda a: a.reshape(1, -1)
    h, nrm = _hyfilt(L, z, w1p, row(lp['hy_b1']), row(lp['hy_freq']), lp['hy_w2'], row(lp['hy_b2']),
                     lp['hy_w3'], dec)
    fre, fim = _hyspec(L, fc, fs, h, nrm)
    pc = _hypre(p, B, L, lp['hy_conv'].reshape(3, 3, W).transpose(1, 0, 2))
    bias = lp['hy_bias'].reshape(2, 1, W)
    yre, yim = _hyfwd(pc, 0, B, L, fc, fs, fre, fim, 0)
    zmid = _hyinv(yre, yim, B, L, fc, fst, pc, 0, pc, 1, bias, 0)
    yre, yim = _hyfwd(zmid, 0, B, L, fc, fs, fre, fim, 1)
    return _hyinv(yre, yim, B, L, fc, fst, zmid, 0, pc, 2, bias, 1)


def _mlstm_kernel(qf, kf, vf, gf, qb, kb, vb, gb, gbias, c0, n0, m0,
                  hf_o, hb_o, c_o, n_o, m_o, c_s, n_s, m_s, *, nc, nb):
    j = pl.program_id(1)
    T = ML_CHUNK

    @pl.when(j == 0)
    def _():
        c_s[...] = c0[...]
        n_s[...] = n0[...]
        m_s[...] = m0[...]

    lane = lax.broadcasted_iota(jnp.int32, (T, 128), 1)
    is_f = ((lane & 4) != 0) & (lane < 16)
    row = lax.broadcasted_iota(jnp.int32, (T, T), 0)
    col = lax.broadcasted_iota(jnp.int32, (T, T), 1)
    masks = (row >= col, row <= col)
    refs = ((qf, kf, vf, gf), (qb, kb, vb, gb))
    grp = [(d, i) for d in range(2) for i in range(nb)]
    Gs = []
    for d, i in grp:
        g = refs[d][3][i] + gbias[...]
        Gs.append(jnp.where(is_f, jnp.minimum(g, 0.0) - jnp.log1p(jnp.exp(-jnp.abs(g))), g))
    Bcs = [_sel_mm(masks[d].astype(BF16), G) for (d, i), G in zip(grp, Gs)]
    sr = lax.broadcasted_iota(jnp.int32, (128, NH * 128), 0)
    sc = lax.broadcasted_iota(jnp.int32, (128, NH * 128), 1) >> 7
    units = []
    for gi, (d, i) in enumerate(grp):
        G, Bc = Gs[gi], Bcs[gi]
        GT, BcT = G.T, Bc.T
        Bcol = _mm_sel(Bc, (sr == sc + (d * 8 + 4)).astype(BF16), 3)
        Icol = _mm_sel(G, (sr == sc + d * 8).astype(BF16), 3)
        q32 = refs[d][0][i] * (HD ** -0.5)
        q = q32.astype(BF16)
        k = refs[d][1][i]
        v = refs[d][2][i].astype(BF16)
        for h in range(NH):
            cf, ci = d * 8 + 4 + h, d * 8 + h
            sl = slice(h * HD, (h + 1) * HD)
            b_col = Bcol[:, h * 128:(h + 1) * 128]
            units.append(dict(
                d=d, i=i, u=d * NH + h, q=q[:, sl], q32=q32[:, sl], k=k[:, sl], v=v[:, sl],
                b_col=b_col, b_row=BcT[cf:cf + 1, :], i_col=Icol[:, h * 128:(h + 1) * 128], i_row=GT[ci:ci + 1, :],
                b_tot=b_col[T - 1:T, :] if d == 0 else b_col[0:1, :]))
    for un in units:
        un['m_prev'] = m_s[un['i'], un['u']:un['u'] + 1, :]
        un['C'] = c_s[un['i'], un['u']]
        un['n'] = n_s[un['i'], un['u']:un['u'] + 1, :]
    for un in units:
        un['qk'] = _mm_nt(un['q'], un['k'], False)
    for un in units:
        un['qC'] = _mm(un['q'], un['C'], False)
    for un in units:
        logd = jnp.where(masks[un['d']], un['b_col'] - un['b_row'] + un['i_row'], -jnp.inf)
        inter = un['b_col'] + un['m_prev']
        m_t = jnp.maximum(inter, jnp.max(logd, axis=-1, keepdims=True))
        un['s'] = un['qk'] * jnp.exp(logd - m_t)
        un['w_in'] = jnp.exp(inter - m_t)
        un['floor'] = jnp.exp(-m_t)
        logw = un['b_tot'] - un['b_col'] + un['i_col']
        m_new = jnp.maximum(un['b_tot'] + un['m_prev'], jnp.max(logw, axis=0, keepdims=True))
        un['kw'] = un['k'] * jnp.exp(logw - m_new)[:, 0:HD]
        un['decay'] = jnp.exp(un['b_tot'] + un['m_prev'] - m_new)[:, 0:HD]
        un['m_new'] = m_new
    for un in units:
        un['sv'] = _mm(un['s'], un['v'], False)
    for un in units:
        un['kv'] = _mm(un['kw'].T, un['v'], False)
    hs = {}
    for un in units:
        i, u = un['i'], un['u']
        num = un['sv'] + un['w_in'][:, 0:HD] * un['qC']
        den = jnp.sum(un['s'], axis=-1, keepdims=True) \
            + un['w_in'] * jnp.sum(un['q32'] * un['n'], axis=-1, keepdims=True)
        hs.setdefault((un['d'], i), []).append(num / jnp.maximum(jnp.abs(den), un['floor'])[:, 0:HD])
        c_s[i, u] = un['decay'] * un['C'] + un['kv']
        n_s[i, u:u + 1, :] = un['decay'] * un['n'] + jnp.sum(un['kw'], axis=0, keepdims=True)
        m_s[i, u:u + 1, :] = un['m_new']
    for (d, i), parts in hs.items():
        (hf_o, hb_o)[d][i] = jnp.concatenate(parts, axis=1)

    @pl.when(j == nc - 1)
    def _():
        c_o[...] = c_s[...]
        n_o[...] = n_s[...]
        m_o[...] = m_s[...]


def _mlstm_scan(p, B, L, gbias, c0, n0, m0):
    T = ML_CHUNK
    nc = L // T
    nb = min(B, 4)
    p3 = p.reshape(B, L, PC)
    fwd = lambda cb: (lambda b, j: (b, j, cb))
    bwd = lambda cb: (lambda b, j: (b, nc - 1 - j, cb))
    blk = lambda f, cb, wd=W: pl.BlockSpec((nb, T, wd), f(cb))
    cq, ck, cv, cg = C_ML // W, C_ML // W + 1, C_ML // W + 2, C_MLG // 128
    st = lambda shape: pl.BlockSpec((nb,) + shape, lambda b, j: (b,) + (0,) * len(shape))
    hf, hb, C, n, m = pl.pallas_call(
        functools.partial(_mlstm_kernel, nc=nc, nb=nb),
        grid=(B // nb, nc),
        in_specs=[blk(fwd, cq), blk(fwd, ck), blk(fwd, cv), blk(fwd, cg, 128),
                  blk(bwd, cq), blk(bwd, ck), blk(bwd, cv), blk(bwd, cg, 128),
                  pl.BlockSpec((1, 128), lambda b, j: (0, 0)),
                  st((8, HD, HD)), st((8, HD)), st((8, 128))],
        out_specs=[pl.BlockSpec((nb, T, W), lambda b, j: (b, j, 0)),
                   pl.BlockSpec((nb, T, W), lambda b, j: (b, nc - 1 - j, 0)),
                   st((8, HD, HD)), st((8, HD)), st((8, 128))],
        out_shape=[jax.ShapeDtypeStruct((B, L, W), F32), jax.ShapeDtypeStruct((B, L, W), F32),
                   jax.ShapeDtypeStruct((B, 8, HD, HD), F32), jax.ShapeDtypeStruct((B, 8, HD), F32),
                   jax.ShapeDtypeStruct((B, 8, 128), F32)],
        scratch_shapes=[pltpu.VMEM((nb, 8, HD, HD), F32), pltpu.VMEM((nb, 8, HD), F32),
                        pltpu.VMEM((nb, 8, 128), F32)],
        compiler_params=_cparams(("arbitrary", "arbitrary")),
        name="mlstm",
    )(p3, p3, p3, p3, p3, p3, p3, p3, gbias, c0, n0, m0)
    return hf.reshape(B * L, W), hb.reshape(B * L, W), C, n, m


def _shifted(x, prev8, next8, first, last):
    T = x.shape[0]
    rid = lax.broadcasted_iota(jnp.int32, x.shape, 0)
    pr = jnp.where(first, 0.0, prev8[7:8, :])
    nx = jnp.where(last, 0.0, next8[0:1, :])
    xp = jnp.where(rid == 0, pr, pltpu.roll(x, 1, axis=0))
    xn = jnp.where(rid == T - 1, nx, pltpu.roll(x, T - 1, axis=0))
    return xp, xn


def _rwprep_kernel(*refs, L, T):
    xs, prevs, nexts = refs[0:4], refs[4:8], refs[8:12]
    mu, w0, w2, a0, a2, g2, kkw, kaw, rkw = refs[12:21]
    r_o, v_o, kk_o, g_o, bon_o, lw0_o, lw1_o, kt0_o, kt1_o, ba0_o, ba1_o = refs[21:]
    i = pl.program_id(0)
    first = (i * T) % L == 0
    last = ((i + 1) * T) % L == 0
    pf = []
    for c in range(4):
        x = xs[c][...]
        xp, xn = _shifted(x, prevs[c][...], nexts[c][...], first, last)
        pf.append(x + mu[:, c * W:(c + 1) * W] * (0.5 * (xp + xn) - x))
    r, k, v, misc = pf
    lw, la, lg = misc[:, 0:64], misc[:, 64:128], misc[:, 128:256]
    ones = _head_ones(W)
    g_o[...] = _mm3(jax.nn.sigmoid(lg), g2[...])
    kq = k * kkw[...]
    kk = kq / jnp.maximum(jnp.sqrt(_mm_sel(kq * kq, ones)), 1e-12)
    tw = jnp.tanh(lw)
    r_o[...] = r
    v_o[...] = v
    kk_o[...] = kk
    bon = jnp.zeros_like(r)
    for d, (lw_o, kt_o, ba_o) in enumerate(((lw0_o, kt0_o, ba0_o), (lw1_o, kt1_o, ba1_o))):
        lw_o[...] = -RW_DECAY_SCALE * jax.nn.sigmoid(w0[d:d + 1, :] + _mm3(tw, w2[d]))
        a = jax.nn.sigmoid(a0[d:d + 1, :] + _mm3(la, a2[d]))
        kt = k * (1.0 + (a - 1.0) * kaw[...])
        kt_o[...] = kt
        ba_o[...] = kk * a
        bon = bon + _mm_sel(r * kt * rkw[...], ones) * v
    bon_o[...] = bon


def _rwprep(p, B, L, lp):
    n = B * L
    T = min(L, 512)
    nb8 = n // 8
    c0 = C_RW // W
    cur = lambda c: pl.BlockSpec((T, W), lambda i: (i, c0 + c))
    prv = lambda c: pl.BlockSpec((8, W), lambda i: (jnp.maximum(i * (T // 8) - 1, 0), c0 + c))
    nxt = lambda c: pl.BlockSpec((8, W), lambda i: (jnp.minimum((i + 1) * (T // 8), nb8 - 1), c0 + c))
    const = lambda shape: pl.BlockSpec(shape, lambda i: (0,) * len(shape))
    row = lambda a: a.reshape(1, -1)
    out = pl.BlockSpec((T, W), lambda i: (i, 0))
    return pl.pallas_call(
        functools.partial(_rwprep_kernel, L=L, T=T),
        grid=(n // T,),
        in_specs=[cur(c) for c in range(4)] + [prv(c) for c in range(4)] + [nxt(c) for c in range(4)]
        + [const((1, 4 * W)), const((2, W)), const((2, 64, W)), const((2, W)), const((2, 64, W)),
           const((128, W)), const((1, W)), const((1, W)), const((1, W))],
        out_specs=[out] * 11,
        out_shape=[jax.ShapeDtypeStruct((n, W), F32)] * 11,
        compiler_params=_cparams(("arbitrary",)),
        name="rwprep",
    )(*([p] * 12), row(lp['rw_mu']), lp['rw_w0'], lp['rw_w2'], lp['rw_a0'], lp['rw_a2'], lp['rw_g2'],
      row(lp['rw_kk']), row(lp['rw_ka']), row(lp['rw_rk']))


def _rwscan_kernel(rf, vf, kkf, lwf, ktf, baf, rb, vb, kkb, lwb, ktb, bab, s0,
                   yf_o, yb_o, s_o, s_s, *, nc, nb):
    j = pl.program_id(1)
    T = RW_CHUNK

    @pl.when(j == 0)
    def _():
        s_s[...] = s0[...]

    row = lax.broadcasted_iota(jnp.int32, (T, T), 0)
    col = lax.broadcasted_iota(jnp.int32, (T, T), 1)
    incl = (row >= col, row <= col)
    strict = (row > col, row < col)
    refs = ((rf, vf, kkf, lwf, ktf, baf), (rb, vb, kkb, lwb, ktb, bab))
    grp = [(d, i) for d in range(2) for i in range(nb)]
    lws = [refs[d][3][i] for d, i in grp]
    lcs = [_sel_mm(incl[d].astype(BF16), lw) for (d, i), lw in zip(grp, lws)]
    units = []
    for (d, i), lw, lc in zip(grp, lws, lcs):
        r_ref, v_ref, kk_ref, _, kt_ref, ba_ref = refs[d]
        gam = jnp.exp(lc)
        ginv = jnp.exp(-lc)
        at = -kk_ref[i] * jnp.exp(lc - lw)
        ar = jnp.concatenate([at, r_ref[i] * gam], axis=0).astype(BF16)
        bk = jnp.concatenate([ba_ref[i] * ginv, kt_ref[i] * ginv], axis=0).astype(BF16)
        v = v_ref[i]
        vb16 = v.astype(BF16)
        gtot = gam[T - 1:T, :] if d == 0 else gam[0:1, :]
        for h in range(NH):
            sl = slice(h * HD, (h + 1) * HD)
            units.append(dict(d=d, i=i, u=d * NH + h, ar=ar[:, sl], bk=bk[:, sl], v=v[:, sl], vb=vb16[:, sl],
                              gtot=gtot[:, sl]))
    for un in units:
        un['S'] = s_s[un['i'], un['u']]
    for un in units:
        un['P'] = lax.dot_general(un['ar'], un['bk'], _NT, preferred_element_type=F32)
    for un in units:
        un['AS'] = _mm_nt(un['ar'], un['S'], False)
    for un in units:
        P, d = un['P'], un['d']
        un['A_ab'] = jnp.where(strict[d], P[0:T, 0:T], 0.0)
        un['A_rb'] = jnp.where(incl[d], P[T:2 * T, 0:T], 0.0).astype(BF16)
        un['A_k'] = jnp.concatenate([jnp.where(strict[d], P[0:T, T:2 * T], 0.0),
                                     jnp.where(incl[d], P[T:2 * T, T:2 * T], 0.0)], axis=0).astype(BF16)
    for un in units:
        un['AV'] = jnp.dot(un['A_k'], un['vb'], preferred_element_type=F32)
    pair = ((row >> 1) == (col >> 1)) & (row != col)
    eye = (row == col).astype(F32)
    for un in units:
        un['X'] = eye + jnp.where(pair, un['A_ab'], 0.0)
    bs = 2
    while bs < T:
        sh = bs.bit_length() - 1
        pair = ((row >> (sh + 1)) == (col >> (sh + 1))) & ((row >> sh) != (col >> sh))
        for un in units:
            un['Xb'] = un['X'].astype(BF16)
            un['XN'] = jnp.dot(un['Xb'], jnp.where(pair, un['A_ab'], 0.0).astype(BF16), preferred_element_type=F32)
        for un in units:
            un['X'] = un['X'] + jnp.dot(un['XN'].astype(BF16), un['Xb'], preferred_element_type=F32)
        bs *= 2
    for un in units:
        un['U'] = _mm(un['X'], un['AS'][0:T] + un['AV'][0:T], False)
    for un in units:
        un['Y'] = un['AS'][T:2 * T] + un['AV'][T:2 * T] + jnp.dot(un['A_rb'], un['U'].astype(BF16),
                                                                  preferred_element_type=F32)
    for un in units:
        UV = jnp.concatenate([un['U'], un['v']], axis=0)
        s_s[un['i'], un['u']] = (un['S'] + _mm(UV.T, un['bk'], False)) * un['gtot']
    ys = {}
    for un in units:
        ys.setdefault((un['d'], un['i']), []).append(un['Y'])
    for (d, i), parts in ys.items():
        (yf_o, yb_o)[d][i] = jnp.concatenate(parts, axis=1)

    @pl.when(j == nc - 1)
    def _():
        s_o[...] = s_s[...]


def _rwscan(prep, B, L, s0):
    T = RW_CHUNK
    nc = L // T
    nb = min(B, 4)
    r, v, kk, lw0, lw1, kt0, kt1, ba0, ba1 = [a.reshape(B, L, W) for a in prep]
    fwd = pl.BlockSpec((nb, T, W), lambda b, j: (b, j, 0))
    bwd = pl.BlockSpec((nb, T, W), lambda b, j: (b, nc - 1 - j, 0))
    st = pl.BlockSpec((nb, 8, HD, HD), lambda b, j: (b, 0, 0, 0))
    yf, yb, S = pl.pallas_call(
        functools.partial(_rwscan_kernel, nc=nc, nb=nb),
        grid=(B // nb, nc),
        in_specs=[fwd] * 6 + [bwd] * 6 + [st],
        out_specs=[fwd, bwd, st],
        out_shape=[jax.ShapeDtypeStruct((B, L, W), F32), jax.ShapeDtypeStruct((B, L, W), F32),
                   jax.ShapeDtypeStruct((B, 8, HD, HD), F32)],
        scratch_shapes=[pltpu.VMEM((nb, 8, HD, HD), F32)],
        compiler_params=_cparams(("arbitrary", "arbitrary")),
        name="rwscan",
    )(r, v, kk, lw0, kt0, ba0, r, v, kk, lw1, kt1, ba1, s0)
    return yf.reshape(B * L, W), yb.reshape(B * L, W), S


def _rms(x, gain):
    ms = _mm_sel(x * x, _head_ones(x.shape[1])) * (1.0 / HD)
    return x * lax.rsqrt(ms + 1e-6) * gain


def _rope(x, cos, sin):
    wd = x.shape[1]
    lane = lax.broadcasted_iota(jnp.int32, x.shape, 1)
    partner = jnp.where((lane & 16) == 0, pltpu.roll(x, wd - 16, axis=1), pltpu.roll(x, 16, axis=1))
    return x * cos + partner * sin


def _attn_kernel(*refs, rope, cached, tq, kchunk):
    if rope:
        q_ref, k_ref, v_ref, qn, kn, cosq, sinq, cosk, sink = refs[:9]
        rest = refs[9:]
    else:
        q_ref, k_ref, v_ref, qn, kn = refs[:5]
        rest = refs[5:]
    if cached:
        kc_ref, vc_ref = rest[:2]
        rest = rest[2:]
    o_ref, kn_o, k_s, v_s = rest
    t = pl.program_id(1)
    L = k_ref.shape[0]
    Lk = k_s.shape[0]

    def with_ones(v):
        ones = jnp.ones((v.shape[0], HD), BF16)
        return jnp.concatenate([v[:, 0:HD].astype(BF16), ones, v[:, HD:2 * HD].astype(BF16), ones], axis=1)

    @pl.when(t == 0)
    def _():
        k = _rms(k_ref[...], kn[...])
        if rope:
            k = _rope(k, cosk[...], sink[...])
        kn_o[...] = k
        k_s[0:L, :] = k.astype(BF16)
        v_s[0:L, :] = with_ones(v_ref[...])
        if cached:
            k_s[L:Lk, :] = kc_ref[0].astype(BF16)
            v_s[L:Lk, :] = with_ones(vc_ref[0])

    q = _rms(q_ref[...], qn[...])
    if rope:
        q = _rope(q, cosq[...], sinq[...])
    q = (q * (HD ** -0.5 * math.log2(math.e))).astype(BF16)
    for jkv in range(2):
        sl = slice(jkv * HD, (jkv + 1) * HD)
        q2 = jnp.concatenate([q[:, (2 * jkv) * HD:(2 * jkv + 1) * HD],
                              q[:, (2 * jkv + 1) * HD:(2 * jkv + 2) * HD]], axis=0)
        m = jnp.full((2 * tq, 1), -jnp.inf, F32)
        acc = jnp.zeros((2 * tq, 2 * HD), F32)
        for c in range(0, Lk, kchunk):
            n = min(kchunk, Lk - c)
            kc = k_s[c:c + n, sl]
            vc = v_s[c:c + n, 2 * jkv * HD:2 * (jkv + 1) * HD]
            s = _mm_nt(q2, kc, hi=False)
            m_new = jnp.maximum(m, jnp.max(s, axis=-1, keepdims=True))
            pr = jnp.exp2(s - m_new)
            acc = jnp.exp2(m - m_new) * acc + jnp.dot(pr.astype(BF16), vc, preferred_element_type=F32)
            m = m_new
        o = acc[:, 0:HD] / acc[:, HD:2 * HD]
        o_ref[:, (2 * jkv) * HD:(2 * jkv + 1) * HD] = o[0:tq]
        o_ref[:, (2 * jkv + 1) * HD:(2 * jkv + 2) * HD] = o[tq:2 * tq]


def _attn(p, B, L, qn, kn, rope_tabs=None, cache=None):
    tq = min(L, 512)
    nt = L // tq
    kchunk = min(L, 1024)
    rope, cached = rope_tabs is not None, cache is not None
    cq, ck, cv = C_AT // W, (C_AT + W) // 128, (C_AT + W) // 128 + 1
    const = lambda shape: pl.BlockSpec(shape, lambda b, t: (0, 0))
    in_specs = [pl.BlockSpec((tq, W), lambda b, t: (b * nt + t, cq)),
                pl.BlockSpec((L, 128), lambda b, t: (b, ck)),
                pl.BlockSpec((L, 128), lambda b, t: (b, cv)),
                const((1, W)), const((1, 128))]
    args = [p, p, p, qn, kn]
    Lk = L
    if rope:
        cos, sin = rope_tabs
        in_specs += [pl.BlockSpec((tq, W), lambda b, t: (t, 0)), pl.BlockSpec((tq, W), lambda b, t: (t, 0)),
                     const((L, 128)), const((L, 128))]
        args += [cos, sin, cos, sin]
    if cached:
        kc, vc = cache
        P = kc.shape[1]
        Lk = L + P
        in_specs += [pl.BlockSpec((1, P, 128), lambda b, t: (b, 0, 0))] * 2
        args += [kc, vc]
    return pl.pallas_call(
        functools.partial(_attn_kernel, rope=rope, cached=cached, tq=tq, kchunk=kchunk),
        grid=(B, nt),
        in_specs=in_specs,
        out_specs=[pl.BlockSpec((tq, W), lambda b, t: (b * nt + t, 0)),
                   pl.BlockSpec((L, 128), lambda b, t: (b, 0))],
        out_shape=[jax.ShapeDtypeStruct((B * L, W), F32), jax.ShapeDtypeStruct((B * L, 128), F32)],
        scratch_shapes=[pltpu.VMEM((Lk, 128), BF16), pltpu.VMEM((Lk, 256), BF16)],
        compiler_params=_cparams(("arbitrary", "arbitrary")),
        name="attn",
    )(*args)


def _rope_tables(L):
    t = jnp.arange(L)
    pos_r = (t // GRID_W).astype(F32)[:, None]
    pos_c = (t % GRID_W).astype(F32)[:, None]
    nf = HD // 4
    inv = ROPE_THETA ** (-jnp.arange(nf, dtype=F32) / nf)[None, :]
    ar, ac = pos_r * inv, pos_c * inv
    cos = jnp.concatenate([jnp.cos(ar), jnp.cos(ar), jnp.cos(ac), jnp.cos(ac)], axis=1)
    sin = jnp.concatenate([-jnp.sin(ar), jnp.sin(ar), -jnp.sin(ac), jnp.sin(ac)], axis=1)
    return jnp.tile(cos, (1, NH)), jnp.tile(sin, (1, NH))


def _layer(x, B, L, mod8, base, rows_per_cond, lp, consts, ctx):
    p = _inproj(x, mod8, lp['w_in_p'], rows_per_cond, base)
    y_hy = _hyena(p, B, L, consts['hy'], lp)

    if ctx is None:
        c0 = jnp.zeros((B, 8, HD, HD), F32)
        n0 = jnp.zeros((B, 8, HD), F32)
        m0 = jnp.zeros((B, 8, 128), F32)
        s0 = jnp.zeros((B, 8, HD, HD), F32)
        cache = None
    else:
        kc, vc, c0, n0, m0, s0 = ctx
        P = kc.shape[1]
        cache = (kc.reshape(B, P, 128), vc.reshape(B, P, 128))
        c0 = c0.reshape(B, 8, HD, HD)
        n0 = n0.reshape(B, 8, HD)
        m0 = jnp.broadcast_to(m0.reshape(B, 8, 1), (B, 8, 128))
        s0 = s0.reshape(B, 8, HD, HD)

    gbias = jnp.pad(lp['ml_gate_b'], (0, 128 - 4 * NH)).reshape(1, 128)
    hf, hb, C, n, m = _mlstm_scan(p, B, L, gbias, c0, n0, m0)

    r, v, kk, g, bon, lw0, lw1, kt0, kt1, ba0, ba1 = _rwprep(p, B, L, lp)
    yf, yb, S = _rwscan((r, v, kk, lw0, lw1, kt0, kt1, ba0, ba1), B, L, s0)

    qn = jnp.tile(lp['at_qn'], NH).reshape(1, W)
    kn = jnp.tile(lp['at_kn'], 2).reshape(1, 128)
    y_at, k_new = _attn(p, B, L, qn, kn, consts.get('rope'), cache)
    v_new = p[:, C_AT + W + 128:C_AT + W + 256]

    row = lambda a: a.reshape(1, D)
    vec = lambda a: a.reshape(1, W)
    ys = (y_hy, hf, hb, p, vec(lp['ml_norm_g']), yf, yb, bon, g, vec(lp['rw_ln_g']), vec(lp['rw_ln_b']), y_at)
    x = _outmlp(ys, x, mod8, lp['w_out_b'], row(lp['ln1_g']), row(lp['ln1_b']),
                lp['mlp_w1_b'], lp['mlp_w2_b'], row(lp['ln2_g']), row(lp['ln2_b']), rows_per_cond, base)
    state = (k_new.reshape(B, L, 2, HD), v_new.reshape(B, L, 2, HD), C.reshape(B, 2, NH, HD, HD),
             n.reshape(B, 2, NH, HD), m[:, :, 0].reshape(B, 2, NH), S.reshape(B, 2, NH, HD, HD))
    return x, state


def _permute_w_in(w_in):
    hy = w_in[:, :, 0:768]
    ml = w_in[:, :, 768:1792]
    mlg = w_in[:, :, 1792:1808]
    rw = w_in[:, :, 1808:2832]
    at = w_in[:, :, 2832:3344]
    mlg = jnp.pad(mlg, ((0, 0), (0, 0), (0, 128 - 16)))
    return jnp.concatenate([hy, ml, rw, at, mlg], axis=-1).astype(BF16)


def kernel(x_prompt, x_sample, cache_attn_k, cache_attn_v, state_mlstm_C, state_mlstm_n, state_mlstm_m, state_rwkv_S, c, c_ctx, w_mod, b_mod, w_in, hy_conv, hy_w1, hy_b1, hy_freq, hy_w2, hy_b2, hy_w3, hy_bias, ml_gate_b, ml_norm_g, rw_mu, rw_w0, rw_w2, rw_a0, rw_a2, rw_g2, rw_kk, rw_ka, rw_rk, rw_ln_g, rw_ln_b, at_qn, at_kn, w_out, ln1_g, ln1_b, mlp_w1, mlp_w2, ln2_g, ln2_b):
    Bp, Lp, _ = x_prompt.shape
    Bs, Ls, _ = x_sample.shape
    params = {
        'w_in_p': _permute_w_in(w_in), 'hy_conv': hy_conv, 'hy_w1': hy_w1, 'hy_b1': hy_b1, 'hy_freq': hy_freq,
        'hy_w2': hy_w2, 'hy_b2': hy_b2, 'hy_w3': hy_w3, 'hy_bias': hy_bias,
        'ml_gate_b': ml_gate_b, 'ml_norm_g': ml_norm_g,
        'rw_mu': rw_mu, 'rw_w0': rw_w0, 'rw_w2': rw_w2, 'rw_a0': rw_a0, 'rw_a2': rw_a2, 'rw_g2': rw_g2,
        'rw_kk': rw_kk, 'rw_ka': rw_ka, 'rw_rk': rw_rk, 'rw_ln_g': rw_ln_g, 'rw_ln_b': rw_ln_b,
        'at_qn': at_qn, 'at_kn': at_kn, 'w_out_b': w_out.astype(BF16), 'ln1_g': ln1_g, 'ln1_b': ln1_b,
        'mlp_w1_b': mlp_w1.astype(BF16), 'mlp_w2_b': mlp_w2.astype(BF16), 'ln2_g': ln2_g, 'ln2_b': ln2_b,
    }
    cvec = jnp.concatenate([c_ctx[None, :], c, jnp.zeros((8 - 1 - Bs, D), F32)], axis=0)
    mod = _mod_all(cvec, w_mod, b_mod).reshape(DEPTH, 8, 1, 6 * D)

    consts_p = {'hy': _hyena_consts(Lp) + _dft_consts(Lp)}
    consts_s = {'hy': _hyena_consts(Ls) + _dft_consts(Ls), 'rope': _rope_tables(Ls)}

    x = x_prompt.reshape(Bp * Lp, D)
    states = []
    for l in range(DEPTH):
        lp = {name: arr[l] for name, arr in params.items()}
        x, st = _layer(x, Bp, Lp, mod[l], 0, Bp * Lp, lp, consts_p, None)
        states.append(st)
    y_prompt = x.reshape(Bp, Lp, D)
    outs = [jnp.stack([st[i] for st in states], axis=1) for i in range(6)]

    x = x_sample.reshape(Bs * Ls, D)
    for l in range(DEPTH):
        lp = {name: arr[l] for name, arr in params.items()}
        ctx = (cache_attn_k[:, l], cache_attn_v[:, l], state_mlstm_C[:, l], state_mlstm_n[:, l],
               state_mlstm_m[:, l], state_rwkv_S[:, l])
        x, _ = _layer(x, Bs, Ls, mod[l], 1, Ls, lp, consts_s, ctx)
    y_sample = x.reshape(Bs, Ls, D)
    return (y_prompt, y_sample) + tuple(outs)
```
